```python
import jax
import jax.numpy as jnp
from jax import lax
import numpy as np

D_MODEL = 1024
BATCH = 4
SEQ = 4096
DEPTH = 4
DEC_BATCH = 32
DEC_SEQ = 8
PAST_LEN = 8192
PAGE_SIZE = 128

MIX_WIDTH = D_MODEL
HEAD_DIM = 64
NSA_HEADS = 8
NSA_KV_HEADS = 2
NSA_GROUP = NSA_HEADS // NSA_KV_HEADS
NSA_WIDTH = NSA_HEADS * HEAD_DIM
KV_WIDTH = NSA_KV_HEADS * HEAD_DIM
CMP_STRIDE = 16
CMP_BLOCK = 2 * CMP_STRIDE
CMP_HIDDEN = 4 * HEAD_DIM
SLC_BLOCK = 64
SLC_TOP_N = 16
WINDOW = 512
SLC_QBLK = 64
WIN_QBLK = 128
FORCED_SCORE = 1.0e4
N_KV_SLOTS = 4
HG_WIDTH = MIX_WIDTH - NSA_WIDTH
HG_HEADS = 4
HG_DK = HG_WIDTH // HG_HEADS
HG_DV = HG_DK
HG_CHUNK = 32
D_FF = 4 * D_MODEL
ROPE_THETA = 10000.0
EPS = 1e-6
PROJ_SIZES = [NSA_WIDTH] + [KV_WIDTH] * 6 + [3 * NSA_HEADS] + [HG_WIDTH] * 4
IN_WIDTH = sum(PROJ_SIZES)

kernel_name = 'nsa_hgrn2_parallel_hybrid_step'


def rms_norm(x, g):
    xf = x.astype(jnp.float32)
    y = xf * lax.rsqrt(jnp.mean(xf * xf, axis=-1, keepdims=True) + EPS)
    return (y * g.astype(jnp.float32)).astype(x.dtype)


def rope(x, pos):
    half = HEAD_DIM // 2
    inv_freq = ROPE_THETA ** (-jnp.arange(half, dtype=jnp.float32) / half)
    ang = pos.astype(jnp.float32)[:, None] * inv_freq[None, :]
    cos = jnp.cos(ang)[None, :, None, :]
    sin = jnp.sin(ang)[None, :, None, :]
    xf = x.astype(jnp.float32)
    x1, x2 = xf[..., :half], xf[..., half:]
    return jnp.concatenate([x1 * cos - x2 * sin, x2 * cos + x1 * sin], axis=-1).astype(x.dtype)


def masked_softmax(s, mask):
    s = jnp.where(mask, s.astype(jnp.float32), -jnp.inf)
    m = jnp.max(s, axis=-1, keepdims=True)
    m = jnp.where(jnp.isfinite(m), m, 0.0)
    e = jnp.exp(s - m)
    return e / jnp.maximum(jnp.sum(e, axis=-1, keepdims=True), 1e-30)


def split_proj(p):
    cuts = [int(c) for c in np.cumsum(PROJ_SIZES)[:-1]]
    return jnp.split(p, cuts, axis=-1)


def compress(rows, pos_emb, w1, b1, w2):
    B, L, hk, d = rows.shape
    ch = rows.reshape(B, L // CMP_STRIDE, CMP_STRIDE, hk, d)
    blk = jnp.concatenate([ch[:, :-1], ch[:, 1:]], axis=2) + pos_emb[None, None, :, None, :]
    flat = jnp.moveaxis(blk, 2, 3).reshape(B, -1, hk, CMP_BLOCK * d)
    return jax.nn.gelu(flat @ w1 + b1) @ w2


def window_band(qg, k, v):
    B, T = qg.shape[:2]
    nb = T // WIN_QBLK
    nsh = WINDOW // WIN_QBLK
    def band(a):
        ap = jnp.pad(a, ((0, 0), (WINDOW, 0), (0, 0), (0, 0)))
        ap = ap.reshape(B, nb + nsh, WIN_QBLK, NSA_KV_HEADS, HEAD_DIM)
        return jnp.concatenate([ap[:, j:j + nb] for j in range(nsh + 1)], axis=2)
    kb, vb = band(k), band(v)
    qb = qg.reshape(B, nb, WIN_QBLK, NSA_KV_HEADS, NSA_GROUP, HEAD_DIM)
    q_pos = jnp.arange(T).reshape(nb, WIN_QBLK)
    k_pos = (jnp.arange(nb) * WIN_QBLK - WINDOW)[:, None] + jnp.arange((nsh + 1) * WIN_QBLK)[None, :]
    diff = q_pos[:, :, None] - k_pos[:, None, :]
    mask = (diff >= 0) & (diff < WINDOW) & (k_pos[:, None, :] >= 0)
    s = jnp.einsum('bcqgnd,bcsgd->bcqgns', qb, kb) * (HEAD_DIM ** -0.5)
    p = masked_softmax(s, mask[None, :, :, None, None, :])
    o = jnp.einsum('bcqgns,bcsgd->bcqgnd', p.astype(vb.dtype), vb)
    return o.reshape(B, T, NSA_KV_HEADS, NSA_GROUP, HEAD_DIM)


def window_dense(qg, q_pos, k, v, k_pos):
    s = jnp.einsum('btgnd,bsgd->btgns', qg, k) * (HEAD_DIM ** -0.5)
    diff = q_pos[:, None] - k_pos[None, :]
    mask = (diff >= 0) & (diff < WINDOW)
    p = masked_softmax(s, mask[None, :, None, None, :])
    return jnp.einsum('btgns,bsgd->btgnd', p.astype(v.dtype), v)


def nsa_attention(q, q_rot, q_pos, gates, kv_all, win_k, win_v, win_pos,
                  kn_cmp, cmp_pos, cmp_w1, cmp_b1, cmp_w2):
    B, Tq = q.shape[:2]
    L = kv_all.shape[1]
    Lp = -(-L // SLC_BLOCK) * SLC_BLOCK
    kv_all = jnp.pad(kv_all, ((0, 0), (0, Lp - L), (0, 0), (0, 0), (0, 0)))
    k_cmp, v_cmp, k_slc, v_slc = kv_all[:, :, 0], kv_all[:, :, 1], kv_all[:, :, 2], kv_all[:, :, 3]
    scale = HEAD_DIM ** -0.5
    qg = q.reshape(B, Tq, NSA_KV_HEADS, NSA_GROUP, HEAD_DIM)
    qrg = q_rot.reshape(B, Tq, NSA_KV_HEADS, NSA_GROUP, HEAD_DIM)

    kc = rms_norm(compress(k_cmp, cmp_pos[0], cmp_w1[0], cmp_b1[0], cmp_w2[0]), kn_cmp)
    vc = compress(v_cmp, cmp_pos[1], cmp_w1[1], cmp_b1[1], cmp_w2[1])
    n_cmp = kc.shape[1]
    c_end = jnp.arange(n_cmp) * CMP_STRIDE + CMP_BLOCK - 1
    c_mask = c_end[None, :] <= q_pos[:, None]
    s = jnp.einsum('btgnd,bcgd->btgnc', qg, kc) * scale
    p_cmp = masked_softmax(s, c_mask[None, :, None, None, :])
    o_cmp = jnp.einsum('btgnc,bcgd->btgnd', p_cmp.astype(vc.dtype), vc)

    n_sel = Lp // SLC_BLOCK
    imp = jnp.pad(jnp.sum(p_cmp, axis=3), ((0, 0), (0, 0), (0, 0), (0, 1)))
    imp = jnp.sum(imp.reshape(B, Tq, NSA_KV_HEADS, n_sel, SLC_BLOCK // CMP_STRIDE), axis=-1)
    blk = jnp.arange(n_sel)[None, :]
    cur = (q_pos // SLC_BLOCK)[:, None]
    forced = (blk == 0) | (blk == cur) | (blk == cur - 1)
    visible = blk * SLC_BLOCK <= q_pos[:, None]
    imp = jnp.where(forced[None, :, None, :], FORCED_SCORE, imp)
    imp = jnp.where(visible[None, :, None, :], imp, -jnp.inf)
    n_top = min(SLC_TOP_N, n_sel)
    _, sel = lax.top_k(imp, n_top)

    kb = k_slc.reshape(B, n_sel, SLC_BLOCK, NSA_KV_HEADS, HEAD_DIM).transpose(0, 3, 1, 2, 4)
    vb = v_slc.reshape(B, n_sel, SLC_BLOCK, NSA_KV_HEADS, HEAD_DIM).transpose(0, 3, 1, 2, 4)
    qbs = SLC_QBLK if Tq % SLC_QBLK == 0 else Tq
    nqb = Tq // qbs
    gather = jax.vmap(jax.vmap(lambda a, i: a[i]))

    def slc_block(args):
        qb, ib, pb = args
        ii = ib.transpose(0, 2, 1, 3).reshape(B, NSA_KV_HEADS, qbs * n_top)
        kg = gather(kb, ii).reshape(B, NSA_KV_HEADS, qbs, n_top * SLC_BLOCK, HEAD_DIM)
        vg = gather(vb, ii).reshape(B, NSA_KV_HEADS, qbs, n_top * SLC_BLOCK, HEAD_DIM)
        kpos = (ii[..., None] * SLC_BLOCK + jnp.arange(SLC_BLOCK)).reshape(B, NSA_KV_HEADS, qbs, -1)
        mask = (kpos <= pb[None, None, :, None]).transpose(0, 2, 1, 3)[:, :, :, None, :]
        sc = jnp.einsum('bqgnd,bgqsd->bqgns', qb, kg) * scale
        pr = masked_softmax(sc, mask)
        return jnp.einsum('bqgns,bgqsd->bqgnd', pr.astype(vg.dtype), vg)

    def to_blocks(a):
        return jnp.moveaxis(a.reshape(B, nqb, qbs, *a.shape[2:]), 1, 0)

    o_slc = lax.map(slc_block, (to_blocks(qrg), to_blocks(sel), q_pos.reshape(nqb, qbs)))
    o_slc = jnp.moveaxis(o_slc, 0, 1).reshape(B, Tq, NSA_KV_HEADS, NSA_GROUP, HEAD_DIM)

    if win_pos is None:
        o_win = window_band(qrg, win_k, win_v)
    else:
        o_win = window_dense(qrg, q_pos, win_k, win_v, win_pos)

    shp = (B, Tq, NSA_HEADS, HEAD_DIM)
    o = (o_cmp.reshape(shp) * gates[..., 0:1] + o_slc.reshape(shp) * gates[..., 1:2]
         + o_win.reshape(shp) * gates[..., 2:3])
    return o.reshape(B, Tq, NSA_WIDTH)


def hgrn2(q, f_pre, i, g, lb, s0, out_gain):
    B, T, _ = q.shape
    C = HG_CHUNK if T % HG_CHUNK == 0 else T
    nc = T // C
    def heads(a):
        return a.reshape(B, nc, C, HG_HEADS, HG_DK).transpose(0, 1, 3, 2, 4)
    z = f_pre.astype(jnp.float32)
    lbf = lb.astype(jnp.float32)
    log_f = jnp.logaddexp(jnp.log(lbf), jnp.log1p(-lbf) + jax.nn.log_sigmoid(z))
    key = (1.0 - lbf) * jax.nn.sigmoid(-z)
    qh = heads(jax.nn.silu(q.astype(jnp.float32)))
    kh = heads(key)
    vh = heads(i.astype(jnp.float32))
    b = jnp.cumsum(heads(log_f), axis=3)
    qe = qh * jnp.exp(b)
    ke = kh * jnp.exp(-b)
    causal = jnp.tril(jnp.ones((C, C), dtype=bool))
    a = jnp.where(causal, jnp.einsum('bnhcd,bnhsd->bnhcs', qe, ke), 0.0)
    o_intra = jnp.einsum('bnhcs,bnhse->bnhce', a, vh)
    b_last = b[:, :, :, -1:, :]
    kd = kh * jnp.exp(b_last - b)
    decay = jnp.exp(b_last[:, :, :, 0, :])

    def step(S, inp):
        qe_n, kd_n, v_n, dec_n = inp
        o_n = jnp.einsum('bhcd,bhde->bhce', qe_n, S)
        S = dec_n[..., None] * S + jnp.einsum('bhcd,bhce->bhde', kd_n, v_n)
        return S, o_n

    s_fin, o_inter = lax.scan(step, s0.astype(jnp.float32),
                              (jnp.moveaxis(qe, 1, 0), jnp.moveaxis(kd, 1, 0),
                               jnp.moveaxis(vh, 1, 0), jnp.moveaxis(decay, 1, 0)))
    o = o_intra + jnp.moveaxis(o_inter, 0, 1)
    o = o.transpose(0, 1, 3, 2, 4).reshape(B, T, HG_HEADS, HG_DV)
    o = rms_norm(o, out_gain) * jax.nn.sigmoid(g.astype(jnp.float32)).reshape(B, T, HG_HEADS, HG_DV)
    return o.reshape(B, T, HG_WIDTH).astype(q.dtype), s_fin


def run_trunk(x, pos, nsa_cache, page_table, win_cache, hg_state,
              attn_norm_g, w_in, q_norm_g, k_norm_g, cmp_pos, cmp_w1, cmp_b1, cmp_w2,
              hg_lb_param, hg_norm_g, w_out, mlp_norm_g, w_up, w_down):
    prompt = nsa_cache is None
    B, T, _ = x.shape
    lb_all = jnp.cumsum(jax.nn.softmax(hg_lb_param.astype(jnp.float32), axis=0), axis=0)
    lb_all = lb_all - lb_all[0]

    def kv_heads(a):
        return a.reshape(B, T, NSA_KV_HEADS, HEAD_DIM)

    out_nsa, out_win, out_hg = [], [], []
    for l in range(DEPTH):
        h = rms_norm(x, attn_norm_g[l])
        (q, k_cmp, v_cmp, k_slc, v_slc, k_win, v_win, gate_pre,
         hq, hf, hi, hg) = split_proj(h @ w_in[l])
        q = rms_norm(q.reshape(B, T, NSA_HEADS, HEAD_DIM), q_norm_g[l])
        q_rot = rope(q, pos)
        k_slc = rope(rms_norm(kv_heads(k_slc), k_norm_g[l, 1]), pos)
        k_win = rope(rms_norm(kv_heads(k_win), k_norm_g[l, 2]), pos)
        v_win = kv_heads(v_win)
        nsa_rows = jnp.stack([kv_heads(k_cmp), kv_heads(v_cmp), k_slc, kv_heads(v_slc)], axis=2)
        win_rows = jnp.stack([k_win, v_win], axis=2)
        gates = jax.nn.sigmoid(gate_pre).reshape(B, T, NSA_HEADS, 3)
        if prompt:
            kv_all = nsa_rows
            wk, wv, wpos = k_win, v_win, None
            s0 = jnp.zeros((B, HG_HEADS, HG_DK, HG_DV), jnp.float32)
        else:
            past = nsa_cache[l, page_table].reshape(B, -1, N_KV_SLOTS, NSA_KV_HEADS, HEAD_DIM)
            kv_all = jnp.concatenate([past, nsa_rows], axis=1)
            buf = win_cache[l]
            n_buf = buf.shape[1]
            wk = jnp.concatenate([buf[:, :, 0], k_win], axis=1)
            wv = jnp.concatenate([buf[:, :, 1], v_win], axis=1)
            wpos = jnp.concatenate([pos[0] - n_buf + jnp.arange(n_buf, dtype=pos.dtype), pos])
            s0 = hg_state[l]
        o_nsa = nsa_attention(q, q_rot, pos, gates, kv_all, wk, wv, wpos, k_norm_g[l, 0],
                              cmp_pos[l], cmp_w1[l], cmp_b1[l], cmp_w2[l])
        o_hg, s_fin = hgrn2(hq, hf, hi, hg, lb_all[l], s0, hg_norm_g[l])
        x = x + jnp.concatenate([o_nsa, o_hg], axis=-1) @ w_out[l]
        h2 = rms_norm(x, mlp_norm_g[l])
        x = x + jnp.square(jax.nn.relu(h2 @ w_up[l])) @ w_down[l]
        out_nsa.append(nsa_rows)
        out_win.append(win_rows[:, T - min(WINDOW, T):] if prompt else win_rows)
        out_hg.append(s_fin)
    return x, jnp.stack(out_nsa), jnp.stack(out_win), jnp.stack(out_hg)


def setup_inputs(seed: int = 0) -> dict:
    key = jax.random.key(seed)
    ks = jax.random.split(key, 24)
    n_pages = PAST_LEN // PAGE_SIZE
    n_used = DEC_BATCH * n_pages
    n_pool = n_used + n_used // 4
    win_buf = min(WINDOW, PAST_LEN)

    def nrm(k, shape, scale):
        return scale * jax.random.normal(k, shape, jnp.float32)

    def gain(k, shape):
        return 1.0 + nrm(k, shape, 0.02)

    page_table = jax.random.permutation(ks[5], n_pool)[:n_used].reshape(DEC_BATCH, n_pages).astype(jnp.int32)
    return {
        'x_prompt': nrm(ks[0], (BATCH, SEQ, D_MODEL), 1.0),
        'x_sample': nrm(ks[1], (DEC_BATCH, DEC_SEQ, D_MODEL), 1.0),
        'cache_nsa_kv': nrm(ks[2], (DEPTH, n_pool, PAGE_SIZE, N_KV_SLOTS, NSA_KV_HEADS, HEAD_DIM), 1.0),
        'cache_win_kv': nrm(ks[3], (DEPTH, DEC_BATCH, win_buf, 2, NSA_KV_HEADS, HEAD_DIM), 1.0),
        'state_hgrn': nrm(ks[4], (DEPTH, DEC_BATCH, HG_HEADS, HG_DK, HG_DV), 0.5),
        'page_table': page_table,
        'attn_norm_g': gain(ks[6], (DEPTH, D_MODEL)),
        'w_in': nrm(ks[7], (DEPTH, D_MODEL, IN_WIDTH), D_MODEL ** -0.5),
        'q_norm_g': gain(ks[8], (DEPTH, HEAD_DIM)),
        'k_norm_g': gain(ks[9], (DEPTH, 3, HEAD_DIM)),
        'cmp_pos': nrm(ks[10], (DEPTH, 2, CMP_BLOCK, HEAD_DIM), 0.1),
        'cmp_w1': nrm(ks[11], (DEPTH, 2, CMP_BLOCK * HEAD_DIM, CMP_HIDDEN), (CMP_BLOCK * HEAD_DIM) ** -0.5),
        'cmp_b1': nrm(ks[12], (DEPTH, 2, CMP_HIDDEN), 0.01),
        'cmp_w2': nrm(ks[13], (DEPTH, 2, CMP_HIDDEN, HEAD_DIM), CMP_HIDDEN ** -0.5),
        'hg_lb_param': nrm(ks[14], (DEPTH, HG_WIDTH), 0.1),
        'hg_norm_g': gain(ks[15], (DEPTH, HG_DV)),
        'w_out': nrm(ks[16], (DEPTH, MIX_WIDTH, D_MODEL), MIX_WIDTH ** -0.5),
        'mlp_norm_g': gain(ks[17], (DEPTH, D_MODEL)),
        'w_up': nrm(ks[18], (DEPTH, D_MODEL, D_FF), D_MODEL ** -0.5),
        'w_down': nrm(ks[19], (DEPTH, D_FF, D_MODEL), D_FF ** -0.5),
    }


def reference(x_prompt, x_sample, cache_nsa_kv, cache_win_kv, state_hgrn, page_table,
              attn_norm_g, w_in, q_norm_g, k_norm_g, cmp_pos, cmp_w1, cmp_b1, cmp_w2,
              hg_lb_param, hg_norm_g, w_out, mlp_norm_g, w_up, w_down):
    past_len = page_table.shape[1] * cache_nsa_kv.shape[2]
    pos_prompt = jnp.arange(x_prompt.shape[1], dtype=jnp.int32)
    pos_sample = past_len + jnp.arange(x_sample.shape[1], dtype=jnp.int32)
    y_prompt, p_nsa, p_win, p_hg = run_trunk(
        x_prompt, pos_prompt, None, None, None, None,
        attn_norm_g, w_in, q_norm_g, k_norm_g, cmp_pos, cmp_w1, cmp_b1, cmp_w2,
        hg_lb_param, hg_norm_g, w_out, mlp_norm_g, w_up, w_down)
    y_sample, s_nsa, s_win, s_hg = run_trunk(
        x_sample, pos_sample, cache_nsa_kv, page_table, cache_win_kv, state_hgrn,
        attn_norm_g, w_in, q_norm_g, k_norm_g, cmp_pos, cmp_w1, cmp_b1, cmp_w2,
        hg_lb_param, hg_norm_g, w_out, mlp_norm_g, w_up, w_down)
    return (y_prompt, y_sample, p_nsa, p_win, p_hg, s_nsa, s_win, s_hg)
```

```python
import functools

import numpy as np
import jax
import jax.numpy as jnp
from jax import lax
from jax.experimental import pallas as pl
from jax.experimental.pallas import tpu as pltpu

F32 = jnp.float32
BF16 = jnp.bfloat16

D_MODEL = 1024
HEAD_DIM = 64
NSA_HEADS = 8
KV_HEADS = 2
GROUP = NSA_HEADS // KV_HEADS
NSA_WIDTH = NSA_HEADS * HEAD_DIM
KV_WIDTH = KV_HEADS * HEAD_DIM
CMP_STRIDE = 16
CMP_HIDDEN = 4 * HEAD_DIM
SLC_BLOCK = 64
SLC_TOP_N = 16
WINDOW = 512
FORCED_SCORE = 1.0e4
HG_HEADS = 4
HG_DK = 128
HG_WIDTH = HG_HEADS * HG_DK
HG_CHUNK = 32
D_FF = 4 * D_MODEL
ROPE_THETA = 10000.0
EPS = 1e-6
SCALE = HEAD_DIM ** -0.5
NEG = -1e30

LANES = 128
PA_W = NSA_WIDTH + 6 * KV_WIDTH + LANES
COL_KC, COL_VC, COL_KS, COL_VS, COL_KW, COL_VW = (NSA_WIDTH + i * KV_WIDTH for i in range(6))
COL_GATE = NSA_WIDTH + 6 * KV_WIDTH
VMEM_LIMIT = 56 * 1024 * 1024


def _cparams(sem):
    return pltpu.CompilerParams(dimension_semantics=sem, vmem_limit_bytes=VMEM_LIMIT)


def _dot(a, b):
    return jnp.dot(a, b, preferred_element_type=F32)


def _dot_nt(a, b):
    return lax.dot_general(a, b, (((1,), (1,)), ((), ())), preferred_element_type=F32)


def _split_dot(a, b, nt=False):
    hi = a.astype(BF16)
    lo = (a - hi.astype(F32)).astype(BF16)
    f = _dot_nt if nt else _dot
    return f(hi, b) + f(lo, b)


def _head_norm(v, bd, gain):
    ss = _split_dot(v * v, bd)
    return v * lax.rsqrt(ss * (1.0 / HEAD_DIM) + EPS) * gain


def _rope(v, cos, sin, first_half):
    sw = jnp.where(first_half, pltpu.roll(v, LANES - HEAD_DIM // 2, 1), pltpu.roll(v, HEAD_DIM // 2, 1))
    return v * cos + sw * sin


def _sigmoid(x):
    return 1.0 / (1.0 + jnp.exp(-x))


def _proj_body(x_ref, g_ref, w_ref, cos_ref, sin_ref, gq_ref, gk_ref, bd_ref,
               qc_ref, qr_ref, nsa_ref, win_ref, kvb_ref, gate_ref, hgin_ref):
    x = x_ref[...]
    ms = jnp.mean(x * x, axis=-1, keepdims=True)
    h = (x * lax.rsqrt(ms + EPS) * g_ref[...]).astype(BF16)
    pa = _dot(h, w_ref[:, :PA_W])
    hgin_ref[...] = _dot(h, w_ref[:, PA_W:])
    cos = cos_ref[...]
    sin = sin_ref[...]
    bd = bd_ref[...]
    lane = lax.broadcasted_iota(jnp.int32, cos.shape, 1)
    first_half = (lane % HEAD_DIM) < (HEAD_DIM // 2)
    gq = gq_ref[...]
    for c in range(NSA_WIDTH // LANES):
        sl = slice(LANES * c, LANES * (c + 1))
        qn = _head_norm(pa[:, sl], bd, gq) * SCALE
        qc_ref[:, sl] = qn.astype(BF16)
        qr_ref[:, sl] = _rope(qn, cos, sin, first_half).astype(BF16)
    kc = pa[:, COL_KC:COL_KC + KV_WIDTH]
    vc = pa[:, COL_VC:COL_VC + KV_WIDTH]
    ks = _rope(_head_norm(pa[:, COL_KS:COL_KS + KV_WIDTH], bd, gk_ref[0:1, :]), cos, sin, first_half)
    vs = pa[:, COL_VS:COL_VS + KV_WIDTH]
    kw = _rope(_head_norm(pa[:, COL_KW:COL_KW + KV_WIDTH], bd, gk_ref[1:2, :]), cos, sin, first_half)
    vw = pa[:, COL_VW:COL_VW + KV_WIDTH]
    for i, a in enumerate((kc, vc, ks, vs)):
        nsa_ref[:, LANES * i:LANES * (i + 1)] = a
    win_ref[:, 0:LANES] = kw
    win_ref[:, LANES:2 * LANES] = vw
    for i, a in enumerate((kc, vc, ks, vs, kw, vw)):
        kvb_ref[:, LANES * i:LANES * (i + 1)] = a.astype(BF16)
    gate_ref[...] = _sigmoid(pa[:, COL_GATE:COL_GATE + LANES])


def _proj(x, g, w, cos, sin, gq, gk, bd, tn):
    n = x.shape[0]
    nt = cos.shape[0] // tn
    wtot = w.shape[1]
    row = lambda i: (i, 0)
    const = lambda i: (0, 0)
    outs = [
        jax.ShapeDtypeStruct((n, NSA_WIDTH), BF16),
        jax.ShapeDtypeStruct((n, NSA_WIDTH), BF16),
        jax.ShapeDtypeStruct((n, 4 * KV_WIDTH), F32),
        jax.ShapeDtypeStruct((n, 2 * KV_WIDTH), F32),
        jax.ShapeDtypeStruct((n, 6 * KV_WIDTH), BF16),
        jax.ShapeDtypeStruct((n, LANES), F32),
        jax.ShapeDtypeStruct((n, 4 * HG_WIDTH), F32),
    ]
    return pl.pallas_call(
        _proj_body,
        grid=(n // tn,),
        in_specs=[
            pl.BlockSpec((tn, D_MODEL), row),
            pl.BlockSpec((1, D_MODEL), const),
            pl.BlockSpec((D_MODEL, wtot), const),
            pl.BlockSpec((tn, LANES), lambda i: (i % nt, 0)),
            pl.BlockSpec((tn, LANES), lambda i: (i % nt, 0)),
            pl.BlockSpec((1, LANES), const),
            pl.BlockSpec((2, LANES), const),
            pl.BlockSpec((LANES, LANES), const),
        ],
        out_specs=[pl.BlockSpec((tn, o.shape[1]), row) for o in outs],
        out_shape=outs,
        compiler_params=_cparams(("parallel",)),
        name="proj",
    )(x, g, w, cos, sin, gq, gk, bd)


def _gelu_tanh(x):
    return 0.5 * x * (1.0 + jnp.tanh(0.7978845608028654 * (x + 0.044715 * (x * x * x))))


def _fill_flat(load_rows, flat_ref, nch):
    for j in range(CMP_STRIDE):
        for slot in range(2):
            rows = load_rows(j, slot)
            for g in range(KV_HEADS):
                flat_ref[2 * slot + g, :, HEAD_DIM * j:HEAD_DIM * (j + 1)] = rows[:, HEAD_DIM * g:HEAD_DIM * (g + 1)]


def _compress_core(flat_ref, nch, nvalid, pe_ref, w1_ref, b1_ref, w2_ref, kn_ref, bd_ref, kc_ref, vc_ref):
    rowid = lax.broadcasted_iota(jnp.int32, (nch, LANES), 0)
    for slot in range(2):
        w1 = w1_ref[slot]
        pe = pe_ref[slot]
        pw = _split_dot(pe, w1)
        const = pw[0:1, :CMP_HIDDEN] + pw[8:9, CMP_HIDDEN:] + b1_ref[slot]
        out = None
        for g in range(KV_HEADS):
            f = flat_ref[2 * slot + g].astype(BF16)
            ab = _dot(f, w1)
            pre = ab[:, :CMP_HIDDEN] + pltpu.roll(ab[:, CMP_HIDDEN:], nch - 1, 0) + const
            hid = _gelu_tanh(pre).astype(BF16)
            o = _dot(hid, w2_ref[slot, g])
            out = o if out is None else out + o
        if slot == 0:
            out = _head_norm(out, bd_ref[...], kn_ref[...])
        out = jnp.where(rowid < nvalid, out, 0.0)
        (kc_ref if slot == 0 else vc_ref)[...] = out.astype(BF16)


def _compress_prompt_body(xk_ref, xv_ref, pe_ref, w1_ref, b1_ref, w2_ref, kn_ref, bd_ref, kc_ref, vc_ref, flat_ref):
    nch = flat_ref.shape[1]
    _fill_flat(lambda j, slot: (xk_ref, xv_ref)[slot][pl.ds(j, nch, stride=CMP_STRIDE), :], flat_ref, nch)
    _compress_core(flat_ref, nch, nch - 1, pe_ref, w1_ref, b1_ref, w2_ref, kn_ref, bd_ref, kc_ref, vc_ref)


def _cmp_weight_specs(nidx):
    z = lambda *a: (0,) * nidx
    return [
        pl.BlockSpec((2, 16, CMP_STRIDE * HEAD_DIM), lambda *a: (0, 0, 0)),
        pl.BlockSpec((2, CMP_STRIDE * HEAD_DIM, 2 * CMP_HIDDEN), lambda *a: (0, 0, 0)),
        pl.BlockSpec((2, 1, CMP_HIDDEN), lambda *a: (0, 0, 0)),
        pl.BlockSpec((2, 2, CMP_HIDDEN, LANES), lambda *a: (0, 0, 0, 0)),
        pl.BlockSpec((1, LANES), lambda *a: (0, 0)),
        pl.BlockSpec((LANES, LANES), lambda *a: (0, 0)),
    ]


def _compress_prompt(nsa, batch, seq, cw):
    nch = seq // CMP_STRIDE
    out = jax.ShapeDtypeStruct((batch, nch, LANES), BF16)
    return pl.pallas_call(
        _compress_prompt_body,
        grid=(batch,),
        in_specs=[pl.BlockSpec((seq, KV_WIDTH), lambda b: (b, 0)),
                  pl.BlockSpec((seq, KV_WIDTH), lambda b: (b, 1))] + _cmp_weight_specs(1),
        out_specs=[pl.BlockSpec((None, nch, LANES), lambda b: (b, 0, 0))] * 2,
        out_shape=[out, out],
        scratch_shapes=[pltpu.VMEM((4, nch, CMP_STRIDE * HEAD_DIM), F32)],
        compiler_params=_cparams(("parallel",)),
        name="compress_prompt",
    )(nsa, nsa, *cw)


def _softmax_masked(s, mask):
    sm = jnp.where(mask, s, NEG)
    m = jnp.max(sm, axis=-1, keepdims=True)
    e = jnp.where(mask, jnp.exp(sm - m), 0.0)
    return e / jnp.maximum(jnp.sum(e, axis=-1, keepdims=True), 1e-30)


def _stack_heads(q, g, rows_f32=False):
    parts = [q[:, HEAD_DIM * (GROUP * g + n):HEAD_DIM * (GROUP * g + n + 1)] for n in range(GROUP)]
    return jnp.concatenate(parts, axis=0)


def _attn_prompt_body(qc_ref, qr_ref, gate_ref, kc_ref, vc_ref, slc_ref, win_ref, poolt_ref, et_ref,
                      o_ref, *, tq, seq, kchunk):
    i = pl.program_id(1)
    q0 = i * tq
    ncmp = kc_ref.shape[0]
    nsel = seq // SLC_BLOCK
    m4 = GROUP * tq
    qc = qc_ref[...]
    qr = qr_ref[...]
    gates = gate_ref[...]
    qpos4 = q0 + lax.broadcasted_iota(jnp.int32, (m4, 1), 0) % tq

    blk = lax.broadcasted_iota(jnp.int32, (nsel, tq), 0)
    qp_l = q0 + lax.broadcasted_iota(jnp.int32, (nsel, tq), 1)
    cur = qp_l // SLC_BLOCK
    forced = (blk == 0) | (blk == cur) | (blk == cur - 1)
    visible = blk * SLC_BLOCK <= qp_l

    for g in range(KV_HEADS):
        hsl = slice(HEAD_DIM * g, HEAD_DIM * (g + 1))
        q4 = _stack_heads(qc, g)
        s = _dot_nt(q4, kc_ref[:, hsl])
        c_end = lax.broadcasted_iota(jnp.int32, (1, ncmp), 1) * CMP_STRIDE + (2 * CMP_STRIDE - 1)
        p = _softmax_masked(s, c_end <= qpos4)
        o_cmp = _dot(p.astype(BF16), vc_ref[:, hsl])
        psum = p[0:tq] + p[tq:2 * tq] + p[2 * tq:3 * tq] + p[3 * tq:4 * tq]

        p_hi = psum.astype(BF16)
        p_lo = (psum - p_hi.astype(F32)).astype(BF16)
        imp_t = _dot_nt(poolt_ref[...], p_hi) + _dot_nt(poolt_ref[...], p_lo)
        score = jnp.where(visible, jnp.where(forced, FORCED_SCORE, imp_t), -1.0)
        rank = jnp.zeros((nsel, tq), F32)
        for k in range(nsel):
            rk = score[k:k + 1, :]
            beats = (rk > score) | ((rk == score) & (blk > k))
            rank = rank + jnp.where(beats, 1.0, 0.0)
        sel_t = jnp.where(visible & (rank < float(SLC_TOP_N)), 1.0, 0.0)
        if nsel < LANES:
            sel_t = jnp.concatenate([sel_t, jnp.zeros((LANES - nsel, tq), F32)], axis=0)
        sel = sel_t.T.astype(BF16)

        q4r = _stack_heads(qr, g)
        nk = (q0 + tq + kchunk - 1) // kchunk

        def slc_step(c, carry):
            m_i, l_i, acc = carry
            k0 = pl.multiple_of(c * kchunk, kchunk)
            kblk = slc_ref[pl.ds(k0, kchunk), hsl]
            vblk = slc_ref[pl.ds(k0, kchunk), KV_WIDTH + HEAD_DIM * g:KV_WIDTH + HEAD_DIM * (g + 1)]
            sc = _dot_nt(q4r, kblk)
            msel = _dot_nt(sel, et_ref[pl.ds(k0, kchunk), :])
            kpos = k0 + lax.broadcasted_iota(jnp.int32, (1, kchunk), 1)
            msel4 = jnp.concatenate([msel] * GROUP, axis=0)
            mask = (msel4 > 0.5) & (kpos <= qpos4)
            sm = jnp.where(mask, sc, NEG)
            m_new = jnp.maximum(m_i, jnp.max(sm, axis=-1, keepdims=True))
            alpha = jnp.exp(m_i - m_new)
            e = jnp.where(mask, jnp.exp(sm - m_new), 0.0)
            l_new = alpha * l_i + jnp.sum(e, axis=-1, keepdims=True)
            acc_new = alpha * acc + _dot(e.astype(BF16), vblk)
            return m_new, l_new, acc_new

        m0 = jnp.full((m4, 1), NEG, F32)
        l0 = jnp.zeros((m4, 1), F32)
        a0 = jnp.zeros((m4, HEAD_DIM), F32)
        _, l_f, acc_f = lax.fori_loop(0, nk, slc_step, (m0, l0, a0))
        o_slc = acc_f / jnp.maximum(l_f, 1e-30)

        wlen = WINDOW + tq
        w0 = pl.multiple_of(jnp.maximum(q0 - WINDOW, 0), tq)
        kwb = win_ref[pl.ds(w0, wlen), hsl]
        vwb = win_ref[pl.ds(w0, wlen), KV_WIDTH + HEAD_DIM * g:KV_WIDTH + HEAD_DIM * (g + 1)]
        sw = _dot_nt(q4r, kwb)
        diff = qpos4 - (w0 + lax.broadcasted_iota(jnp.int32, (1, wlen), 1))
        pw = _softmax_masked(sw, (diff >= 0) & (diff < WINDOW))
        o_win = _dot(pw.astype(BF16), vwb)

        for n in range(GROUP):
            hd = GROUP * g + n
            r = slice(n * tq, (n + 1) * tq)
            o = (o_cmp[r] * gates[:, 3 * hd:3 * hd + 1] + o_slc[r] * gates[:, 3 * hd + 1:3 * hd + 2]
                 + o_win[r] * gates[:, 3 * hd + 2:3 * hd + 3])
            o_ref[:, HEAD_DIM * hd:HEAD_DIM * (hd + 1)] = o.astype(BF16)


def _attn_prompt(qc, qr, gates, kc, vc, kvb, poolt, et, batch, seq, tq, kchunk):
    n = batch * seq
    nq = seq // tq
    ncmp = kc.shape[1]
    nsel = seq // SLC_BLOCK
    body = functools.partial(_attn_prompt_body, tq=tq, seq=seq, kchunk=kchunk)
    return pl.pallas_call(
        body,
        grid=(batch, nq),
        in_specs=[
            pl.BlockSpec((tq, NSA_WIDTH), lambda b, i: (b * nq + i, 0)),
            pl.BlockSpec((tq, NSA_WIDTH), lambda b, i: (b * nq + i, 0)),
            pl.BlockSpec((tq, LANES), lambda b, i: (b * nq + i, 0)),
            pl.BlockSpec((None, ncmp, LANES), lambda b, i: (b, 0, 0)),
            pl.BlockSpec((None, ncmp, LANES), lambda b, i: (b, 0, 0)),
            pl.BlockSpec((seq, 2 * KV_WIDTH), lambda b, i: (b, 1)),
            pl.BlockSpec((seq, 2 * KV_WIDTH), lambda b, i: (b, 2)),
            pl.BlockSpec((nsel, ncmp), lambda b, i: (0, 0)),
            pl.BlockSpec(et.shape, lambda b, i: (0, 0)),
        ],
        out_specs=pl.BlockSpec((tq, NSA_WIDTH), lambda b, i: (b * nq + i, 0)),
        out_shape=jax.ShapeDtypeStruct((n, NSA_WIDTH), BF16),
        compiler_params=_cparams(("parallel", "parallel")),
        name="attn_prompt",
    )(qc, qr, gates, kc, vc, kvb, kvb, poolt, et)


def _hgrn_body(x_ref, lb_ref, gain_ref, s0_ref, o_ref, sfin_ref, st_ref, *, tile, nvalid):
    i = pl.program_id(1)
    nsub = tile // HG_CHUNK

    @pl.when(i == 0)
    def _():
        for hd in range(HG_HEADS):
            st_ref[hd] = s0_ref[hd].T

    row = lax.broadcasted_iota(jnp.int32, (tile, HG_DK), 0)
    rowc = row % HG_CHUNK
    col = lax.broadcasted_iota(jnp.int32, (tile, tile), 1)
    rr = lax.broadcasted_iota(jnp.int32, (tile, tile), 0)
    intra_mask = (rr // HG_CHUNK == col // HG_CHUNK) & (rr >= col)
    lane_t = lax.broadcasted_iota(jnp.int32, (HG_DK, tile), 1)
    valid = row < nvalid

    for hd in range(HG_HEADS):
        sl = lambda k: slice(k * HG_WIDTH + hd * HG_DK, k * HG_WIDTH + (hd + 1) * HG_DK)
        q = x_ref[:, sl(0)]
        z = x_ref[:, sl(1)]
        v = x_ref[:, sl(2)]
        gt = x_ref[:, sl(3)]
        lb = lb_ref[:, hd * HG_DK:(hd + 1) * HG_DK]
        log_sig = jnp.minimum(z, 0.0) - jnp.log(1.0 + jnp.exp(-jnp.abs(z)))
        t1 = jnp.log(1.0 - lb) + log_sig
        a = jnp.broadcast_to(jnp.log(lb), t1.shape)
        log_f = jnp.maximum(a, t1) + jnp.log(1.0 + jnp.exp(-jnp.abs(a - t1)))
        key = (1.0 - lb) * _sigmoid(-z)
        if nvalid < tile:
            log_f = jnp.where(valid, log_f, 0.0)
            key = jnp.where(valid, key, 0.0)
            v = jnp.where(valid, v, 0.0)
        qs = q * _sigmoid(q)
        b = log_f
        step = 1
        while step < HG_CHUNK:
            b = b + jnp.where(rowc >= step, pltpu.roll(b, step, 0), 0.0)
            step *= 2
        qe = (qs * jnp.exp(b)).astype(BF16)
        ke = (key * jnp.exp(-b)).astype(BF16)
        vb = v.astype(BF16)
        a_mat = jnp.where(intra_mask, _dot_nt(qe, ke), 0.0)
        o_acc = _dot(a_mat.astype(BF16), vb)
        vt = v.T
        b_lasts = [b[(c + 1) * HG_CHUNK - 1:(c + 1) * HG_CHUNK, :] for c in range(nsub)]
        b_end = jnp.broadcast_to(b_lasts[0], b.shape)
        for c in range(1, nsub):
            b_end = jnp.where(row >= c * HG_CHUNK, b_lasts[c], b_end)
        kd = (key * jnp.exp(b_end - b)).astype(BF16)
        st = st_ref[hd]
        o_parts = []
        for c in range(nsub):
            r = slice(c * HG_CHUNK, (c + 1) * HG_CHUNK)
            o_parts.append(_dot_nt(qe[r], st.astype(BF16)))
            vtc = jnp.where(lane_t // HG_CHUNK == c, vt, 0.0).astype(BF16)
            st = st * jnp.exp(b_lasts[c]) + _dot(vtc, kd)
        st_ref[hd] = st
        o = o_acc + jnp.concatenate(o_parts, axis=0)
        ms = jnp.mean(o * o, axis=-1, keepdims=True)
        o = o * lax.rsqrt(ms + EPS) * gain_ref[...] * _sigmoid(gt)
        o_ref[:, hd * HG_DK:(hd + 1) * HG_DK] = o.astype(BF16)

    @pl.when(i == pl.num_programs(1) - 1)
    def _():
        for hd in range(HG_HEADS):
            sfin_ref[hd] = st_ref[hd].T


def _hgrn(hgin, lb, gain, s0, batch, seq, tile, nvalid):
    nt = seq // tile
    body = functools.partial(_hgrn_body, tile=tile, nvalid=nvalid)
    return pl.pallas_call(
        body,
        grid=(batch, nt),
        in_specs=[
            pl.BlockSpec((tile, 4 * HG_WIDTH), lambda b, i: (b * nt + i, 0)),
            pl.BlockSpec((1, HG_WIDTH), lambda b, i: (0, 0)),
            pl.BlockSpec((1, HG_DK), lambda b, i: (0, 0)),
            pl.BlockSpec((None, HG_HEADS, HG_DK, HG_DK), lambda b, i: (b, 0, 0, 0)),
        ],
        out_specs=[
            pl.BlockSpec((tile, HG_WIDTH), lambda b, i: (b * nt + i, 0)),
            pl.BlockSpec((None, HG_HEADS, HG_DK, HG_DK), lambda b, i: (b, 0, 0, 0)),
        ],
        out_shape=[
            jax.ShapeDtypeStruct((batch * seq, HG_WIDTH), BF16),
            jax.ShapeDtypeStruct((batch, HG_HEADS, HG_DK, HG_DK), F32),
        ],
        scratch_shapes=[pltpu.VMEM((HG_HEADS, HG_DK, HG_DK), F32)],
        compiler_params=_cparams(("parallel", "arbitrary")),
        name="hgrn",
    )(hgin, lb, gain, s0)


def _mlp_body(x_ref, on_ref, oh_ref, wo_ref, g_ref, wu_ref, wd_ref, y_ref, *, fchunk):
    x2 = x_ref[...] + _dot(jnp.concatenate([on_ref[...], oh_ref[...]], axis=1), wo_ref[...])
    ms = jnp.mean(x2 * x2, axis=-1, keepdims=True)
    h2 = (x2 * lax.rsqrt(ms + EPS) * g_ref[...]).astype(BF16)
    acc = x2
    for c in range(D_FF // fchunk):
        u = _dot(h2, wu_ref[:, c * fchunk:(c + 1) * fchunk])
        u = jnp.maximum(u, 0.0)
        acc = acc + _dot((u * u).astype(BF16), wd_ref[c * fchunk:(c + 1) * fchunk, :])
    y_ref[...] = acc


def _mlp(x, on, oh, wo, g, wu, wd, tn, fchunk=1024):
    n = x.shape[0]
    row = lambda i: (i, 0)
    const = lambda i: (0, 0)
    single = {}
    return pl.pallas_call(
        functools.partial(_mlp_body, fchunk=fchunk),
        grid=(n // tn,),
        in_specs=[
            pl.BlockSpec((tn, D_MODEL), row),
            pl.BlockSpec((tn, NSA_WIDTH), row),
            pl.BlockSpec((tn, HG_WIDTH), row),
            pl.BlockSpec((D_MODEL, D_MODEL), const, **single),
            pl.BlockSpec((1, D_MODEL), const),
            pl.BlockSpec((D_MODEL, D_FF), const, **single),
            pl.BlockSpec((D_FF, D_MODEL), const, **single),
        ],
        out_specs=pl.BlockSpec((tn, D_MODEL), row),
        out_shape=jax.ShapeDtypeStruct((n, D_MODEL), F32),
        compiler_params=_cparams(("parallel",)),
        name="mlp",
    )(x, on, oh, wo, g, wu, wd)


def _page_copy(cache_ref, pt_ref, b, p, buf_ref, slot, sem_ref, layer, col0, page, slab):
    return pltpu.make_async_copy(
        cache_ref.at[layer, pt_ref[b, p], :, pl.ds(col0 + slab * KV_WIDTH, KV_WIDTH)],
        buf_ref.at[slot, slab, pl.ds(p * page, page), :],
        sem_ref.at[slot])


def _pages_pipeline(cache_ref, pt_ref, buf_ref, sem_ref, layer, col0, page, npages):
    b = pl.program_id(0)
    nb = pl.num_programs(0)
    slot = b % 2

    def start(bb, sl):
        def go(p, c):
            for slab in range(2):
                _page_copy(cache_ref, pt_ref, bb, p, buf_ref, sl, sem_ref, layer, col0, page, slab).start()
            return c
        lax.fori_loop(0, npages, go, 0)

    @pl.when(b == 0)
    def _():
        start(0, 0)

    @pl.when(b + 1 < nb)
    def _():
        start(b + 1, 1 - slot)

    def wait(p, c):
        for slab in range(2):
            _page_copy(cache_ref, pt_ref, b, p, buf_ref, slot, sem_ref, layer, col0, page, slab).wait()
        return c
    lax.fori_loop(0, npages, wait, 0)
    return slot


def _compress_sample_body(pt_ref, cache_ref, pe_ref, w1_ref, b1_ref, w2_ref, kn_ref, bd_ref,
                          kc_ref, vc_ref, buf_ref, flat_ref, sem_ref, *, layer, page, npages):
    slot = _pages_pipeline(cache_ref, pt_ref, buf_ref, sem_ref, layer, 0, page, npages)
    nch = flat_ref.shape[1]
    _fill_flat(lambda j, s: buf_ref[slot, s, pl.ds(j, nch, stride=CMP_STRIDE), :], flat_ref, nch)
    _compress_core(flat_ref, nch, nch - 1, pe_ref, w1_ref, b1_ref, w2_ref, kn_ref, bd_ref, kc_ref, vc_ref)


def _compress_sample(page_table, cache4, cw, layer):
    nb, npages = page_table.shape
    page = cache4.shape[2]
    past = npages * page
    nch = past // CMP_STRIDE
    out = jax.ShapeDtypeStruct((nb, nch, LANES), BF16)
    body = functools.partial(_compress_sample_body, layer=layer, page=page, npages=npages)
    return pl.pallas_call(
        body,
        grid_spec=pltpu.PrefetchScalarGridSpec(
            num_scalar_prefetch=1,
            grid=(nb,),
            in_specs=[pl.BlockSpec(memory_space=pl.ANY)] + _cmp_weight_specs(2),
            out_specs=[pl.BlockSpec((None, nch, LANES), lambda b, pt: (b, 0, 0))] * 2,
            scratch_shapes=[
                pltpu.VMEM((2, 2, past, KV_WIDTH), F32),
                pltpu.VMEM((4, nch, CMP_STRIDE * HEAD_DIM), F32),
                pltpu.SemaphoreType.DMA((2,)),
            ],
        ),
        out_shape=[out, out],
        compiler_params=_cparams(("arbitrary",)),
        name="compress_sample",
    )(page_table, cache4, *cw)


def _two_part_attention(s1, mask1, v1, s2, mask2, v2):
    s1 = jnp.where(mask1, s1, NEG)
    s2 = jnp.where(mask2, s2, NEG)
    m = jnp.maximum(jnp.max(s1, axis=-1, keepdims=True), jnp.max(s2, axis=-1, keepdims=True))
    e1 = jnp.where(mask1, jnp.exp(s1 - m), 0.0)
    e2 = jnp.where(mask2, jnp.exp(s2 - m), 0.0)
    l = jnp.sum(e1, axis=-1, keepdims=True) + jnp.sum(e2, axis=-1, keepdims=True)
    o = _dot(e1.astype(BF16), v1) + _dot(e2.astype(BF16), v2)
    return o / jnp.maximum(l, 1e-30)


def _attn_sample_body(pt_ref, cache_ref, qc_ref, qr_ref, gate_ref, kc_ref, vc_ref, new_ref, winc_ref,
                      pool_ref, e_ref, o_ref, buf_ref, sem_ref, *, layer, page, npages, tnew):
    slot = _pages_pipeline(cache_ref, pt_ref, buf_ref, sem_ref, layer, 2 * KV_WIDTH, page, npages)
    past = npages * page
    ncmp = kc_ref.shape[0]
    nblk = past // SLC_BLOCK
    nlane = LANES * ((nblk + 1 + LANES - 1) // LANES)
    m4 = GROUP * tnew
    qc = qc_ref[...].astype(F32)
    qr = qr_ref[...].astype(F32)
    gates = gate_ref[...]
    new = new_ref[...].astype(F32)
    pad = jnp.zeros((16 - tnew, HEAD_DIM), F32)
    t4 = lax.broadcasted_iota(jnp.int32, (m4, 1), 0) % tnew
    u16 = lax.broadcasted_iota(jnp.int32, (1, 16), 1)
    new_mask = (u16 <= t4) & (u16 < tnew)

    blk = lax.broadcasted_iota(jnp.int32, (tnew, nlane), 1)
    qpos = past + lax.broadcasted_iota(jnp.int32, (tnew, nlane), 0)
    cur = qpos // SLC_BLOCK
    forced = (blk == 0) | (blk == cur) | (blk == cur - 1)
    visible = (blk * SLC_BLOCK <= qpos) & (blk <= nblk)

    for g in range(KV_HEADS):
        hsl = slice(HEAD_DIM * g, HEAD_DIM * (g + 1))
        q4 = _stack_heads(qc, g).astype(BF16)
        s = _dot_nt(q4, kc_ref[:, hsl])
        c_end = lax.broadcasted_iota(jnp.int32, (1, ncmp), 1) * CMP_STRIDE + (2 * CMP_STRIDE - 1)
        p = _softmax_masked(s, c_end <= past + t4)
        o_cmp = _dot(p.astype(BF16), vc_ref[:, hsl])

        imp4 = _split_dot(p, pool_ref[...])
        imp = imp4[0:tnew]
        for n in range(1, GROUP):
            imp = imp + imp4[n * tnew:(n + 1) * tnew]
        if nlane > nblk:
            imp = jnp.concatenate([imp, jnp.zeros((tnew, nlane - nblk), F32)], axis=1)
        score = jnp.where(visible, jnp.where(forced, FORCED_SCORE, imp), -1.0)
        rank = jnp.zeros((tnew, nlane), F32)
        for k in range(nblk + 1):
            ck = score[:, k:k + 1]
            beats = (ck > score) | ((ck == score) & (blk > k))
            rank = rank + jnp.where(beats, 1.0, 0.0)
        sel = jnp.where(visible & (rank < float(SLC_TOP_N)), 1.0, 0.0)

        q4r = _stack_heads(qr, g).astype(BF16)
        ksp = buf_ref[slot, 0, :, hsl].astype(BF16)
        vsp = buf_ref[slot, 1, :, hsl].astype(BF16)
        sel4 = jnp.concatenate([sel[:, :nblk]] * GROUP, axis=0).astype(BF16)
        mpast = _dot(sel4, e_ref[...]) > 0.5
        ks_new = jnp.concatenate([new[:, COL_KS - NSA_WIDTH + HEAD_DIM * g:COL_KS - NSA_WIDTH + HEAD_DIM * (g + 1)], pad], axis=0).astype(BF16)
        vs_new = jnp.concatenate([new[:, COL_VS - NSA_WIDTH + HEAD_DIM * g:COL_VS - NSA_WIDTH + HEAD_DIM * (g + 1)], pad], axis=0).astype(BF16)
        cur_sel4 = jnp.concatenate([sel[:, nblk:nblk + 1]] * GROUP, axis=0) > 0.5
        o_slc = _two_part_attention(_dot_nt(q4r, ksp), mpast, vsp,
                                    _dot_nt(q4r, ks_new), new_mask & cur_sel4, vs_new)

        nbuf = winc_ref.shape[0]
        kwp = winc_ref[:, hsl].astype(BF16)
        vwp = winc_ref[:, KV_WIDTH + HEAD_DIM * g:KV_WIDTH + HEAD_DIM * (g + 1)].astype(BF16)
        kw_new = jnp.concatenate([new[:, COL_KW - NSA_WIDTH + HEAD_DIM * g:COL_KW - NSA_WIDTH + HEAD_DIM * (g + 1)], pad], axis=0).astype(BF16)
        vw_new = jnp.concatenate([new[:, COL_VW - NSA_WIDTH + HEAD_DIM * g:COL_VW - NSA_WIDTH + HEAD_DIM * (g + 1)], pad], axis=0).astype(BF16)
        diff = (t4 + nbuf) - lax.broadcasted_iota(jnp.int32, (1, nbuf), 1)
        o_win = _two_part_attention(_dot_nt(q4r, kwp), (diff >= 0) & (diff < WINDOW), vwp,
                                    _dot_nt(q4r, kw_new), new_mask, vw_new)

        for n in range(GROUP):
            hd = GROUP * g + n
            r = slice(n * tnew, (n + 1) * tnew)
            o = (o_cmp[r] * gates[:, 3 * hd:3 * hd + 1] + o_slc[r] * gates[:, 3 * hd + 1:3 * hd + 2]
                 + o_win[r] * gates[:, 3 * hd + 2:3 * hd + 3])
            o_ref[:, HEAD_DIM * hd:HEAD_DIM * (hd + 1)] = o


def _attn_sample(page_table, cache4, qc, qr, gates, kc, vc, kvb, winc4, pool, expand, layer, tnew):
    nb, npages = page_table.shape
    page = cache4.shape[2]
    past = npages * page
    ncmp = kc.shape[1]
    nblk = past // SLC_BLOCK
    nbuf = winc4.shape[2]
    body = functools.partial(_attn_sample_body, layer=layer, page=page, npages=npages, tnew=tnew)
    per_b = lambda w: pl.BlockSpec((None, tnew, w), lambda b, pt: (b, 0, 0))
    return pl.pallas_call(
        body,
        grid_spec=pltpu.PrefetchScalarGridSpec(
            num_scalar_prefetch=1,
            grid=(nb,),
            in_specs=[
                pl.BlockSpec(memory_space=pl.ANY),
                per_b(NSA_WIDTH), per_b(NSA_WIDTH), per_b(LANES),
                pl.BlockSpec((None, ncmp, LANES), lambda b, pt: (b, 0, 0)),
                pl.BlockSpec((None, ncmp, LANES), lambda b, pt: (b, 0, 0)),
                per_b(6 * KV_WIDTH),
                pl.BlockSpec((None, None, nbuf, 2 * KV_WIDTH), lambda b, pt: (layer, b, 0, 0)),
                pl.BlockSpec((ncmp, nblk), lambda b, pt: (0, 0)),
                pl.BlockSpec((nblk, past), lambda b, pt: (0, 0)),
            ],
            out_specs=per_b(NSA_WIDTH),
            scratch_shapes=[
                pltpu.VMEM((2, 2, past, KV_WIDTH), F32),
                pltpu.SemaphoreType.DMA((2,)),
            ],
        ),
        out_shape=jax.ShapeDtypeStruct((nb, tnew, NSA_WIDTH), F32),
        compiler_params=_cparams(("arbitrary",)),
        name="attn_sample",
    )(page_table, cache4, qc, qr, gates, kc, vc, kvb, winc4, pool, expand)


def _rope_tables(pos):
    half = HEAD_DIM // 2
    inv_freq = ROPE_THETA ** (-jnp.arange(half, dtype=F32) / half)
    ang = pos.astype(F32)[:, None] * inv_freq[None, :]
    cos = jnp.cos(ang)
    sin = jnp.sin(ang)
    cos_t = jnp.tile(jnp.concatenate([cos, cos], axis=-1), (1, LANES // HEAD_DIM))
    sin_t = jnp.tile(jnp.concatenate([-sin, sin], axis=-1), (1, LANES // HEAD_DIM))
    return cos_t, sin_t


def _pack_layer(l, attn_norm_g, w_in, q_norm_g, k_norm_g, cmp_pos, cmp_w1, cmp_b1, cmp_w2,
                hg_norm_g, w_out, mlp_norm_g, w_up, w_down):
    w = w_in[l]
    gate_lo = COL_GATE
    gate_hi = COL_GATE + 3 * NSA_HEADS
    w_gate = jnp.pad(w[:, gate_lo:gate_hi], ((0, 0), (0, LANES - 3 * NSA_HEADS)))
    w_pack = jnp.concatenate([w[:, :gate_lo], w_gate, w[:, gate_hi:]], axis=1).astype(BF16)
    tile2 = lambda v: jnp.tile(v, LANES // HEAD_DIM)[None, :]
    half = CMP_STRIDE * HEAD_DIM
    w1 = cmp_w1[l]
    w1cat = jnp.concatenate([w1[:, :half], w1[:, half:]], axis=-1).astype(BF16)
    pe = cmp_pos[l].reshape(2, 2, 1, half)
    pe16 = jnp.broadcast_to(pe, (2, 2, 8, half)).reshape(2, 16, half)
    w2 = cmp_w2[l]
    zeros = jnp.zeros_like(w2)
    w2pad = jnp.stack([jnp.concatenate([w2, zeros], axis=-1), jnp.concatenate([zeros, w2], axis=-1)], axis=1).astype(BF16)
    return dict(
        g1=attn_norm_g[l][None, :], w_pack=w_pack, gq=tile2(q_norm_g[l]),
        gk=jnp.concatenate([tile2(k_norm_g[l, 1]), tile2(k_norm_g[l, 2])], axis=0),
        cw=(pe16, w1cat, cmp_b1[l][:, None, :], w2pad, tile2(k_norm_g[l, 0])),
        hg_gain=hg_norm_g[l][None, :], wo=w_out[l].astype(BF16), g2=mlp_norm_g[l][None, :],
        wu=w_up[l].astype(BF16), wd=w_down[l].astype(BF16),
    )


def _same_head_matrix():
    i = np.arange(LANES)
    return jnp.asarray((i[:, None] // HEAD_DIM) == (i[None, :] // HEAD_DIM), BF16)


def _pool_matrix(ncmp_rows, nblk):
    c = np.arange(ncmp_rows)
    j = np.arange(nblk)
    return jnp.asarray((c[:, None] // (SLC_BLOCK // CMP_STRIDE)) == j[None, :], BF16)


def _expand_matrix(nblk, nkeys):
    j = np.arange(nblk)
    s = np.arange(nkeys)
    return jnp.asarray(j[:, None] == (s[None, :] // SLC_BLOCK), BF16)


def kernel(x_prompt, x_sample, cache_nsa_kv, cache_win_kv, state_hgrn, page_table, attn_norm_g, w_in,
           q_norm_g, k_norm_g, cmp_pos, cmp_w1, cmp_b1, cmp_w2, hg_lb_param, hg_norm_g, w_out,
           mlp_norm_g, w_up, w_down):
    depth = w_in.shape[0]
    batch, seq, _ = x_prompt.shape
    nb, tnew, _ = x_sample.shape
    npool, page = cache_nsa_kv.shape[1], cache_nsa_kv.shape[2]
    npages = page_table.shape[1]
    past = npages * page
    nbuf = cache_win_kv.shape[2]

    lb_all = jnp.cumsum(jax.nn.softmax(hg_lb_param.astype(F32), axis=0), axis=0)
    lb_all = lb_all - lb_all[0]

    cos_p, sin_p = _rope_tables(jnp.arange(seq, dtype=jnp.int32))
    cos_s, sin_s = _rope_tables(past + jnp.arange(tnew, dtype=jnp.int32))
    cos_s = jnp.tile(cos_s, (nb, 1))
    sin_s = jnp.tile(sin_s, (nb, 1))
    bd = _same_head_matrix()
    nsel_p = seq // SLC_BLOCK
    poolt_p = _pool_matrix(seq // CMP_STRIDE, nsel_p).T
    et_p = _expand_matrix(max(nsel_p, LANES), seq).T
    pool_s = _pool_matrix(past // CMP_STRIDE, past // SLC_BLOCK)
    expand_s = _expand_matrix(past // SLC_BLOCK, past)

    cache4 = cache_nsa_kv.reshape(depth, npool, page, 4 * KV_WIDTH)
    winc4 = cache_win_kv.reshape(depth, nb, nbuf, 2 * KV_WIDTH)
    zero_state = jnp.zeros((batch, HG_HEADS, HG_DK, HG_DK), F32)

    tq = 128
    hg_tile = 128
    xp = x_prompt.reshape(batch * seq, D_MODEL)
    xs = x_sample.reshape(nb * tnew, D_MODEL)
    p_nsa, p_win, p_hg, s_nsa, s_win, s_hg = [], [], [], [], [], []
    for l in range(depth):
        pk = _pack_layer(l, attn_norm_g, w_in, q_norm_g, k_norm_g, cmp_pos, cmp_w1, cmp_b1, cmp_w2,
                         hg_norm_g, w_out, mlp_norm_g, w_up, w_down)
        lb = lb_all[l][None, :]

        qc, qr, nsa, win, kvb, gates, hgin = _proj(xp, pk["g1"], pk["w_pack"], cos_p, sin_p, pk["gq"], pk["gk"], bd, 512)
        kc, vc = _compress_prompt(nsa, batch, seq, pk["cw"] + (bd,))
        o_nsa = _attn_prompt(qc, qr, gates, kc, vc, kvb, poolt_p, et_p, batch, seq, tq, 512)
        o_hg, hfin = _hgrn(hgin, lb, pk["hg_gain"], zero_state, batch, seq, hg_tile, hg_tile)
        xp = _mlp(xp, o_nsa, o_hg, pk["wo"], pk["g2"], pk["wu"], pk["wd"], 512)
        p_nsa.append(nsa.reshape(batch, seq, 4, KV_HEADS, HEAD_DIM))
        p_win.append(win.reshape(batch, seq, 2, KV_HEADS, HEAD_DIM)[:, seq - min(WINDOW, seq):])
        p_hg.append(hfin)

        ns = nb * tnew
        qc, qr, nsa, win, kvb, gates, hgin = _proj(xs, pk["g1"], pk["w_pack"], cos_s, sin_s, pk["gq"], pk["gk"], bd, ns)
        kc, vc = _compress_sample(page_table, cache4, pk["cw"] + (bd,), l)
        r3 = lambda a: a.reshape(nb, tnew, a.shape[-1])
        o_nsa = _attn_sample(page_table, cache4, r3(qc), r3(qr), r3(gates), kc, vc, r3(kvb), winc4,
                             pool_s, expand_s, l, tnew)
        hg_pad = jnp.pad(r3(hgin), ((0, 0), (0, hg_tile - tnew), (0, 0))).reshape(nb * hg_tile, 4 * HG_WIDTH)
        o_hg, hfin = _hgrn(hg_pad, lb, pk["hg_gain"], state_hgrn[l], nb, hg_tile, hg_tile, tnew)
        o_hg = o_hg.reshape(nb, hg_tile, HG_WIDTH)[:, :tnew].reshape(ns, HG_WIDTH)
        xs = _mlp(xs, o_nsa.reshape(ns, NSA_WIDTH).astype(BF16), o_hg, pk["wo"], pk["g2"], pk["wu"], pk["wd"], ns)
        s_nsa.append(nsa.reshape(nb, tnew, 4, KV_HEADS, HEAD_DIM))
        s_win.append(win.reshape(nb, tnew, 2, KV_HEADS, HEAD_DIM))
        s_hg.append(hfin)

    return (xp.reshape(batch, seq, D_MODEL), xs.reshape(nb, tnew, D_MODEL),
            jnp.stack(p_nsa), jnp.stack(p_win), jnp.stack(p_hg),
            jnp.stack(s_nsa), jnp.stack(s_win), jnp.stack(s_hg))
```

```python
import functools

import numpy as np
import jax
import jax.numpy as jnp
from jax import lax
from jax.experimental import pallas as pl
from jax.experimental.pallas import tpu as pltpu

F32 = jnp.float32
BF16 = jnp.bfloat16

D_MODEL = 1024
HEAD_DIM = 64
NSA_HEADS = 8
KV_HEADS = 2
GROUP = NSA_HEADS // KV_HEADS
NSA_WIDTH = NSA_HEADS * HEAD_DIM
KV_WIDTH = KV_HEADS * HEAD_DIM
CMP_STRIDE = 16
CMP_HIDDEN = 4 * HEAD_DIM
SLC_BLOCK = 64
SLC_TOP_N = 16
WINDOW = 512
FORCED_SCORE = 1.0e4
HG_HEADS = 4
HG_DK = 128
HG_WIDTH = HG_HEADS * HG_DK
HG_CHUNK = 32
D_FF = 4 * D_MODEL
ROPE_THETA = 10000.0
EPS = 1e-6
SCALE = HEAD_DIM ** -0.5
NEG = -1e30
BLOCK_CODE = -(2.0 ** 100)

LANES = 128
PA_W = NSA_WIDTH + 6 * KV_WIDTH + LANES
COL_KC, COL_VC, COL_KS, COL_VS, COL_KW, COL_VW = (NSA_WIDTH + i * KV_WIDTH for i in range(6))
COL_GATE = NSA_WIDTH + 6 * KV_WIDTH
VMEM_LIMIT = 56 * 1024 * 1024


def _cparams(sem):
    return pltpu.CompilerParams(dimension_semantics=sem, vmem_limit_bytes=VMEM_LIMIT)


def _dot(a, b):
    return jnp.dot(a, b, preferred_element_type=F32)


def _dot_nt(a, b):
    return lax.dot_general(a, b, (((1,), (1,)), ((), ())), preferred_element_type=F32)


def _split_dot(a, b, nt=False):
    hi = a.astype(BF16)
    lo = (a - hi.astype(F32)).astype(BF16)
    f = _dot_nt if nt else _dot
    return f(hi, b) + f(lo, b)


def _head_norm(v, bd, gain):
    ss = _split_dot(v * v, bd)
    return v * lax.rsqrt(ss * (1.0 / HEAD_DIM) + EPS) * gain


def _rope(v, cos, sin, first_half):
    sw = jnp.where(first_half, pltpu.roll(v, LANES - HEAD_DIM // 2, 1), pltpu.roll(v, HEAD_DIM // 2, 1))
    return v * cos + sw * sin


def _sigmoid(x):
    return 1.0 / (1.0 + jnp.exp(-x))


def _proj_body(x_ref, g_ref, w_ref, cos_ref, sin_ref, code_ref, gq_ref, gk_ref, bd_ref,
               qc_ref, qr_ref, nsa_ref, win_ref, kva_ref, gate_ref, hgin_ref):
    x = x_ref[...]
    ms = jnp.mean(x * x, axis=-1, keepdims=True)
    h = (x * lax.rsqrt(ms + EPS) * g_ref[...]).astype(BF16)
    pa = _dot(h, w_ref[:, :PA_W])
    hgin_ref[...] = _dot(h, w_ref[:, PA_W:])
    cos = cos_ref[...]
    sin = sin_ref[...]
    bd = bd_ref[...]
    lane = lax.broadcasted_iota(jnp.int32, cos.shape, 1)
    first_half = (lane % HEAD_DIM) < (HEAD_DIM // 2)
    gq = gq_ref[...]
    for c in range(NSA_WIDTH // LANES):
        sl = slice(LANES * c, LANES * (c + 1))
        qn = _head_norm(pa[:, sl], bd, gq) * SCALE
        qc_ref[:, sl] = qn.astype(BF16)
        qr_ref[:, sl] = _rope(qn, cos, sin, first_half).astype(BF16)
    kc = pa[:, COL_KC:COL_KC + KV_WIDTH]
    vc = pa[:, COL_VC:COL_VC + KV_WIDTH]
    ks = _rope(_head_norm(pa[:, COL_KS:COL_KS + KV_WIDTH], bd, gk_ref[0:1, :]), cos, sin, first_half)
    vs = pa[:, COL_VS:COL_VS + KV_WIDTH]
    kw = _rope(_head_norm(pa[:, COL_KW:COL_KW + KV_WIDTH], bd, gk_ref[1:2, :]), cos, sin, first_half)
    vw = pa[:, COL_VW:COL_VW + KV_WIDTH]
    for i, a in enumerate((kc, vc, ks, vs)):
        nsa_ref[:, LANES * i:LANES * (i + 1)] = a
    win_ref[:, 0:LANES] = kw
    win_ref[:, LANES:2 * LANES] = vw
    low = lane < HEAD_DIM
    code = code_ref[...]
    aug = []
    for kk, vv, fill in ((ks, vs, code), (kw, vw, 0.0)):
        aug += [jnp.where(low, kk, fill), jnp.where(low, pltpu.roll(kk, HEAD_DIM, 1), fill),
                jnp.where(low, vv, 1.0), jnp.where(low, pltpu.roll(vv, HEAD_DIM, 1), 1.0)]
    for i, a in enumerate(aug):
        kva_ref[:, LANES * i:LANES * (i + 1)] = a.astype(BF16)
    gate_ref[...] = _sigmoid(pa[:, COL_GATE:COL_GATE + LANES])


def _proj(x, g, w, cos, sin, code, gq, gk, bd, tn):
    n = x.shape[0]
    nt = cos.shape[0] // tn
    wtot = w.shape[1]
    row = lambda i: (i, 0)
    const = lambda i: (0, 0)
    outs = [
        jax.ShapeDtypeStruct((n, NSA_WIDTH), BF16),
        jax.ShapeDtypeStruct((n, NSA_WIDTH), BF16),
        jax.ShapeDtypeStruct((n, 4 * KV_WIDTH), F32),
        jax.ShapeDtypeStruct((n, 2 * KV_WIDTH), F32),
        jax.ShapeDtypeStruct((n, 8 * LANES), BF16),
        jax.ShapeDtypeStruct((n, LANES), F32),
        jax.ShapeDtypeStruct((n, 4 * HG_WIDTH), F32),
    ]
    return pl.pallas_call(
        _proj_body,
        grid=(n // tn,),
        in_specs=[
            pl.BlockSpec((tn, D_MODEL), row),
            pl.BlockSpec((1, D_MODEL), const),
            pl.BlockSpec((D_MODEL, wtot), const),
            pl.BlockSpec((tn, LANES), lambda i: (i % nt, 0)),
            pl.BlockSpec((tn, LANES), lambda i: (i % nt, 0)),
            pl.BlockSpec((tn, LANES), lambda i: (i % nt, 0)),
            pl.BlockSpec((1, LANES), const),
            pl.BlockSpec((2, LANES), const),
            pl.BlockSpec((LANES, LANES), const),
        ],
        out_specs=[pl.BlockSpec((tn, o.shape[1]), row) for o in outs],
        out_shape=outs,
        compiler_params=_cparams(("parallel",)),
        name="proj",
    )(x, g, w, cos, sin, code, gq, gk, bd)


def _gelu_tanh(x):
    return 0.5 * x * (1.0 + jnp.tanh(0.7978845608028654 * (x + 0.044715 * (x * x * x))))


def _fill_flat(load_rows, flat_ref, nch):
    for j in range(CMP_STRIDE):
        for slot in range(2):
            rows = load_rows(j, slot)
            for g in range(KV_HEADS):
                flat_ref[2 * slot + g, :, HEAD_DIM * j:HEAD_DIM * (j + 1)] = rows[:, HEAD_DIM * g:HEAD_DIM * (g + 1)]


def _compress_core(flat_ref, nch, nvalid, pe_ref, w1_ref, b1_ref, w2_ref, kn_ref, bd_ref, out_ref):
    rowid = lax.broadcasted_iota(jnp.int32, (nch, LANES), 0)
    low = lax.broadcasted_iota(jnp.int32, (nch, LANES), 1) < HEAD_DIM
    for slot in range(2):
        w1 = w1_ref[slot]
        pe = pe_ref[slot]
        pw = _split_dot(pe, w1)
        const = pw[0:1, :CMP_HIDDEN] + pw[8:9, CMP_HIDDEN:] + b1_ref[slot]
        for g in range(KV_HEADS):
            f = flat_ref[2 * slot + g].astype(BF16)
            ab = _dot(f, w1)
            pre = ab[:, :CMP_HIDDEN] + pltpu.roll(ab[:, CMP_HIDDEN:], nch - 1, 0) + const
            hid = _gelu_tanh(pre).astype(BF16)
            o = _dot(hid, w2_ref[slot])
            if slot == 0:
                o = _head_norm(o, bd_ref[...], kn_ref[...])
            else:
                o = jnp.where(low, o, 1.0)
            out_ref[2 * slot + g] = jnp.where(rowid < nvalid, o, 0.0).astype(BF16)


def _compress_prompt_body(xk_ref, xv_ref, pe_ref, w1_ref, b1_ref, w2_ref, kn_ref, bd_ref, out_ref, flat_ref):
    nch = flat_ref.shape[1]
    _fill_flat(lambda j, slot: (xk_ref, xv_ref)[slot][pl.ds(j, nch, stride=CMP_STRIDE), :], flat_ref, nch)
    _compress_core(flat_ref, nch, nch - 1, pe_ref, w1_ref, b1_ref, w2_ref, kn_ref, bd_ref, out_ref)


def _cmp_weight_specs(nidx):
    z = lambda *a: (0,) * nidx
    return [
        pl.BlockSpec((2, 16, CMP_STRIDE * HEAD_DIM), lambda *a: (0, 0, 0)),
        pl.BlockSpec((2, CMP_STRIDE * HEAD_DIM, 2 * CMP_HIDDEN), lambda *a: (0, 0, 0)),
        pl.BlockSpec((2, 1, CMP_HIDDEN), lambda *a: (0, 0, 0)),
        pl.BlockSpec((2, CMP_HIDDEN, LANES), lambda *a: (0, 0, 0)),
        pl.BlockSpec((1, LANES), lambda *a: (0, 0)),
        pl.BlockSpec((LANES, LANES), lambda *a: (0, 0)),
    ]


def _compress_prompt(nsa, batch, seq, cw):
    nch = seq // CMP_STRIDE
    return pl.pallas_call(
        _compress_prompt_body,
        grid=(batch,),
        in_specs=[pl.BlockSpec((seq, KV_WIDTH), lambda b: (b, 0)),
                  pl.BlockSpec((seq, KV_WIDTH), lambda b: (b, 1))] + _cmp_weight_specs(1),
        out_specs=pl.BlockSpec((None, 4, nch, LANES), lambda b: (b, 0, 0, 0)),
        out_shape=jax.ShapeDtypeStruct((batch, 4, nch, LANES), BF16),
        scratch_shapes=[pltpu.VMEM((4, nch, CMP_STRIDE * HEAD_DIM), F32)],
        compiler_params=_cparams(("parallel",)),
        name="compress_prompt",
    )(nsa, nsa, *cw)


def _softmax_masked(s, mask):
    sm = jnp.where(mask, s, NEG)
    m = jnp.max(sm, axis=-1, keepdims=True)
    e = jnp.where(mask, jnp.exp(sm - m), 0.0)
    return e / jnp.maximum(jnp.sum(e, axis=-1, keepdims=True), 1e-30)


def _stack_heads(q, g, rows_f32=False):
    parts = [q[:, HEAD_DIM * (GROUP * g + n):HEAD_DIM * (GROUP * g + n + 1)] for n in range(GROUP)]
    return jnp.concatenate(parts, axis=0)


def _place_heads(q, g, fill, low):
    parts = []
    for n in range(GROUP):
        hd = GROUP * g + n
        chunk = q[:, LANES * (hd // 2):LANES * (hd // 2 + 1)].astype(F32)
        if hd % 2:
            chunk = pltpu.roll(chunk, HEAD_DIM, 1)
        parts.append(jnp.where(low, chunk, fill).astype(BF16))
    return jnp.concatenate(parts, axis=0)


def _attend(q4, k, v, bias):
    s = _dot_nt(q4, k)
    if bias is not None:
        s = s + bias
    e = jnp.exp(s - jnp.max(s, axis=-1, keepdims=True))
    return _dot(e.astype(BF16), v), e


def _attn_prompt_body(qc_ref, qr_ref, gate_ref, cmp_ref, slc_ref, win_ref, poolt_ref, o_ref, score_ref,
                      *, tq, seq, kchunk):
    i = pl.program_id(1)
    q0 = i * tq
    ncmp = cmp_ref.shape[1]
    nsel = seq // SLC_BLOCK
    m4 = GROUP * tq
    qc = qc_ref[...]
    qr = qr_ref[...]
    gates = gate_ref[...]
    low = lax.broadcasted_iota(jnp.int32, (tq, LANES), 1) < HEAD_DIM
    qpos = q0 + lax.broadcasted_iota(jnp.int32, (tq, 1), 0)
    rep4 = lambda a: jnp.concatenate([a] * GROUP, axis=0)

    c_end = lax.broadcasted_iota(jnp.int32, (1, ncmp), 1) * CMP_STRIDE + (2 * CMP_STRIDE - 1)
    cmp_bias = rep4(jnp.where(c_end <= qpos, 0.0, NEG))
    cmp_any = rep4(jnp.where(qpos >= 2 * CMP_STRIDE - 1, 1.0, 0.0))
    wlen = WINDOW + tq
    w0 = pl.multiple_of(jnp.maximum(q0 - WINDOW, 0), tq)
    diff = qpos - (w0 + lax.broadcasted_iota(jnp.int32, (1, wlen), 1))
    win_bias = rep4(jnp.where((diff >= 0) & (diff < WINDOW), 0.0, NEG))
    nk = (q0 + tq + kchunk - 1) // kchunk
    kpos_d = (nk - 1) * kchunk + lax.broadcasted_iota(jnp.int32, (1, kchunk), 1)
    diag_bias = rep4(jnp.where(kpos_d <= qpos, 0.0, NEG))

    blk = lax.broadcasted_iota(jnp.int32, (nsel, tq), 0)
    qp_l = q0 + lax.broadcasted_iota(jnp.int32, (nsel, tq), 1)
    cur = qp_l // SLC_BLOCK
    forced = (blk == 0) | (blk == cur) | (blk == cur - 1)
    visible = blk * SLC_BLOCK <= qp_l
    nvis = jnp.minimum((q0 + tq + SLC_BLOCK - 1) // SLC_BLOCK, nsel)

    for g in range(KV_HEADS):
        acc_c, e_c = _attend(_place_heads(qc, g, 0.0, low), cmp_ref[g], cmp_ref[KV_HEADS + g], cmp_bias)
        inv_c = cmp_any / acc_c[:, HEAD_DIM:HEAD_DIM + 1]
        o_cmp = acc_c * inv_c
        p = e_c * inv_c
        psum = p[0:tq] + p[tq:2 * tq] + p[2 * tq:3 * tq] + p[3 * tq:4 * tq]

        p_hi = psum.astype(BF16)
        p_lo = (psum - p_hi.astype(F32)).astype(BF16)
        imp_t = _dot_nt(poolt_ref[...], p_hi) + _dot_nt(poolt_ref[...], p_lo)
        score = jnp.where(visible, jnp.where(forced, FORCED_SCORE, imp_t), -1.0)
        score_ref[...] = score

        def rank_step(k, rank):
            rk = score_ref[pl.ds(k, 1), :]
            return rank + jnp.where((rk > score) | ((rk == score) & (blk > k)), 1.0, 0.0)

        rank = lax.fori_loop(0, nvis, rank_step, jnp.zeros((nsel, tq), F32))
        notsel = jnp.where(visible & (rank < float(SLC_TOP_N)), 0.0, 1.0)
        pieces = [jnp.zeros((HEAD_DIM, tq), F32), notsel]
        if nsel < HEAD_DIM:
            pieces.append(jnp.ones((HEAD_DIM - nsel, tq), F32))
        fill = jnp.concatenate(pieces, axis=0).T

        q4r = _place_heads(qr, g, fill, low)

        def slc_step(c, carry, bias):
            m_i, acc = carry
            k0 = pl.multiple_of(c * kchunk, kchunk)
            s = _dot_nt(q4r, slc_ref[pl.ds(k0, kchunk), LANES * g:LANES * (g + 1)])
            if bias is not None:
                s = s + bias
            m_new = jnp.maximum(m_i, jnp.max(s, axis=-1, keepdims=True))
            e = jnp.exp(s - m_new).astype(BF16)
            pv = _dot(e, slc_ref[pl.ds(k0, kchunk), LANES * (KV_HEADS + g):LANES * (KV_HEADS + g + 1)])
            return m_new, jnp.exp(m_i - m_new) * acc + pv

        carry = (jnp.full((m4, 1), NEG, F32), jnp.zeros((m4, LANES), F32))
        carry = lax.fori_loop(0, nk - 1, lambda c, cr: slc_step(c, cr, None), carry)
        _, acc_s = slc_step(nk - 1, carry, diag_bias)
        o_slc = acc_s / acc_s[:, HEAD_DIM:HEAD_DIM + 1]

        acc_w, _ = _attend(q4r, win_ref[pl.ds(w0, wlen), LANES * g:LANES * (g + 1)],
                           win_ref[pl.ds(w0, wlen), LANES * (KV_HEADS + g):LANES * (KV_HEADS + g + 1)], win_bias)
        o_win = acc_w / acc_w[:, HEAD_DIM:HEAD_DIM + 1]

        comb = []
        for n in range(GROUP):
            hd = GROUP * g + n
            r = slice(n * tq, (n + 1) * tq)
            comb.append(o_cmp[r] * gates[:, 3 * hd:3 * hd + 1] + o_slc[r] * gates[:, 3 * hd + 1:3 * hd + 2]
                        + o_win[r] * gates[:, 3 * hd + 2:3 * hd + 3])
        for pair in range(GROUP // 2):
            both = jnp.where(low, comb[2 * pair], pltpu.roll(comb[2 * pair + 1], HEAD_DIM, 1))
            c0 = LANES * ((GROUP // 2) * g + pair)
            o_ref[:, c0:c0 + LANES] = both.astype(BF16)


def _attn_prompt(qc, qr, gates, cmp, kva, poolt, batch, seq, tq, kchunk):
    n = batch * seq
    nq = seq // tq
    ncmp = cmp.shape[2]
    nsel = seq // SLC_BLOCK
    assert nsel <= HEAD_DIM, "block codes use the 64 spare contraction lanes"
    body = functools.partial(_attn_prompt_body, tq=tq, seq=seq, kchunk=kchunk)
    return pl.pallas_call(
        body,
        grid=(batch, nq),
        in_specs=[
            pl.BlockSpec((tq, NSA_WIDTH), lambda b, i: (b * nq + i, 0)),
            pl.BlockSpec((tq, NSA_WIDTH), lambda b, i: (b * nq + i, 0)),
            pl.BlockSpec((tq, LANES), lambda b, i: (b * nq + i, 0)),
            pl.BlockSpec((None, 4, ncmp, LANES), lambda b, i: (b, 0, 0, 0)),
            pl.BlockSpec((seq, 4 * LANES), lambda b, i: (b, 0)),
            pl.BlockSpec((seq, 4 * LANES), lambda b, i: (b, 1)),
            pl.BlockSpec((nsel, ncmp), lambda b, i: (0, 0)),
        ],
        out_specs=pl.BlockSpec((tq, NSA_WIDTH), lambda b, i: (b * nq + i, 0)),
        out_shape=jax.ShapeDtypeStruct((n, NSA_WIDTH), BF16),
        scratch_shapes=[pltpu.VMEM((nsel, tq), F32)],
        compiler_params=_cparams(("parallel", "parallel")),
        name="attn_prompt",
    )(qc, qr, gates, cmp, kva, kva, poolt)


def _hgrn_body(x_ref, lb_ref, gain_ref, s0_ref, o_ref, sfin_ref, st_ref, *, tile, nvalid):
    i = pl.program_id(1)
    nsub = tile // HG_CHUNK

    @pl.when(i == 0)
    def _():
        for hd in range(HG_HEADS):
            st_ref[hd] = s0_ref[hd].T

    row = lax.broadcasted_iota(jnp.int32, (tile, HG_DK), 0)
    rowc = row % HG_CHUNK
    col = lax.broadcasted_iota(jnp.int32, (tile, tile), 1)
    rr = lax.broadcasted_iota(jnp.int32, (tile, tile), 0)
    intra_mask = (rr // HG_CHUNK == col // HG_CHUNK) & (rr >= col)
    lane_t = lax.broadcasted_iota(jnp.int32, (HG_DK, tile), 1)
    valid = row < nvalid

    for hd in range(HG_HEADS):
        sl = lambda k: slice(k * HG_WIDTH + hd * HG_DK, k * HG_WIDTH + (hd + 1) * HG_DK)
        q = x_ref[:, sl(0)]
        z = x_ref[:, sl(1)]
        v = x_ref[:, sl(2)]
        gt = x_ref[:, sl(3)]
        lb = lb_ref[:, hd * HG_DK:(hd + 1) * HG_DK]
        log_sig = jnp.minimum(z, 0.0) - jnp.log(1.0 + jnp.exp(-jnp.abs(z)))
        t1 = jnp.log(1.0 - lb) + log_sig
        a = jnp.broadcast_to(jnp.log(lb), t1.shape)
        log_f = jnp.maximum(a, t1) + jnp.log(1.0 + jnp.exp(-jnp.abs(a - t1)))
        key = (1.0 - lb) * _sigmoid(-z)
        if nvalid < tile:
            log_f = jnp.where(valid, log_f, 0.0)
            key = jnp.where(valid, key, 0.0)
            v = jnp.where(valid, v, 0.0)
        qs = q * _sigmoid(q)
        b = log_f
        step = 1
        while step < HG_CHUNK:
            b = b + jnp.where(rowc >= step, pltpu.roll(b, step, 0), 0.0)
            step *= 2
        qe = (qs * jnp.exp(b)).astype(BF16)
        ke = (key * jnp.exp(-b)).astype(BF16)
        vb = v.astype(BF16)
        a_mat = jnp.where(intra_mask, _dot_nt(qe, ke), 0.0)
        o_acc = _dot(a_mat.astype(BF16), vb)
        vt = v.T
        b_lasts = [b[(c + 1) * HG_CHUNK - 1:(c + 1) * HG_CHUNK, :] for c in range(nsub)]
        b_end = jnp.broadcast_to(b_lasts[0], b.shape)
        for c in range(1, nsub):
            b_end = jnp.where(row >= c * HG_CHUNK, b_lasts[c], b_end)
        kd = (key * jnp.exp(b_end - b)).astype(BF16)
        st = st_ref[hd]
        o_parts = []
        for c in range(nsub):
            r = slice(c * HG_CHUNK, (c + 1) * HG_CHUNK)
            o_parts.append(_dot_nt(qe[r], st.astype(BF16)))
            vtc = jnp.where(lane_t // HG_CHUNK == c, vt, 0.0).astype(BF16)
            st = st * jnp.exp(b_lasts[c]) + _dot(vtc, kd)
        st_ref[hd] = st
        o = o_acc + jnp.concatenate(o_parts, axis=0)
        ms = jnp.mean(o * o, axis=-1, keepdims=True)
        o = o * lax.rsqrt(ms + EPS) * gain_ref[...] * _sigmoid(gt)
        o_ref[:, hd * HG_DK:(hd + 1) * HG_DK] = o.astype(BF16)

    @pl.when(i == pl.num_programs(1) - 1)
    def _():
        for hd in range(HG_HEADS):
            sfin_ref[hd] = st_ref[hd].T


def _hgrn(hgin, lb, gain, s0, batch, seq, tile, nvalid):
    nt = seq // tile
    body = functools.partial(_hgrn_body, tile=tile, nvalid=nvalid)
    return pl.pallas_call(
        body,
        grid=(batch, nt),
        in_specs=[
            pl.BlockSpec((tile, 4 * HG_WIDTH), lambda b, i: (b * nt + i, 0)),
            pl.BlockSpec((1, HG_WIDTH), lambda b, i: (0, 0)),
            pl.BlockSpec((1, HG_DK), lambda b, i: (0, 0)),
            pl.BlockSpec((None, HG_HEADS, HG_DK, HG_DK), lambda b, i: (b, 0, 0, 0)),
        ],
        out_specs=[
            pl.BlockSpec((tile, HG_WIDTH), lambda b, i: (b * nt + i, 0)),
            pl.BlockSpec((None, HG_HEADS, HG_DK, HG_DK), lambda b, i: (b, 0, 0, 0)),
        ],
        out_shape=[
            jax.ShapeDtypeStruct((batch * seq, HG_WIDTH), BF16),
            jax.ShapeDtypeStruct((batch, HG_HEADS, HG_DK, HG_DK), F32),
        ],
        scratch_shapes=[pltpu.VMEM((HG_HEADS, HG_DK, HG_DK), F32)],
        compiler_params=_cparams(("parallel", "arbitrary")),
        name="hgrn",
    )(hgin, lb, gain, s0)


def _mlp_body(x_ref, on_ref, oh_ref, wo_ref, g_ref, wu_ref, wd_ref, y_ref, *, fchunk):
    x2 = x_ref[...] + _dot(jnp.concatenate([on_ref[...], oh_ref[...]], axis=1), wo_ref[...])
    ms = jnp.mean(x2 * x2, axis=-1, keepdims=True)
    h2 = (x2 * lax.rsqrt(ms + EPS) * g_ref[...]).astype(BF16)
    acc = x2
    for c in range(D_FF // fchunk):
        u = _dot(h2, wu_ref[:, c * fchunk:(c + 1) * fchunk])
        u = jnp.maximum(u, 0.0)
        acc = acc + _dot((u * u).astype(BF16), wd_ref[c * fchunk:(c + 1) * fchunk, :])
    y_ref[...] = acc


def _mlp(x, on, oh, wo, g, wu, wd, tn, fchunk=1024):
    n = x.shape[0]
    row = lambda i: (i, 0)
    const = lambda i: (0, 0)
    single = {}
    return pl.pallas_call(
        functools.partial(_mlp_body, fchunk=fchunk),
        grid=(n // tn,),
        in_specs=[
            pl.BlockSpec((tn, D_MODEL), row),
            pl.BlockSpec((tn, NSA_WIDTH), row),
            pl.BlockSpec((tn, HG_WIDTH), row),
            pl.BlockSpec((D_MODEL, D_MODEL), const, **single),
            pl.BlockSpec((1, D_MODEL), const),
            pl.BlockSpec((D_MODEL, D_FF), const, **single),
            pl.BlockSpec((D_FF, D_MODEL), const, **single),
        ],
        out_specs=pl.BlockSpec((tn, D_MODEL), row),
        out_shape=jax.ShapeDtypeStruct((n, D_MODEL), F32),
        compiler_params=_cparams(("parallel",)),
        name="mlp",
    )(x, on, oh, wo, g, wu, wd)


def _pages_pipeline(page_copy, npages):
    b = pl.program_id(0)
    nb = pl.num_programs(0)
    slot = b % 2

    def start(bb, sl):
        def go(p, c):
            page_copy(bb, p, sl).start()
            return c
        lax.fori_loop(0, npages, go, 0)

    @pl.when(b == 0)
    def _():
        start(0, 0)

    @pl.when(b + 1 < nb)
    def _():
        start(b + 1, 1 - slot)

    def wait(p, c):
        page_copy(b, p, slot).wait()
        return c
    lax.fori_loop(0, npages, wait, 0)
    return slot


def _compress_sample_body(pt_ref, cache_ref, pe_ref, w1_ref, b1_ref, w2_ref, kn_ref, bd_ref,
                          out_ref, buf_ref, rows_ref, flat_ref, sem_ref, *, layer, page, npages):
    nfeat = 2 * KV_WIDTH

    def page_copy(b, p, slot):
        return pltpu.make_async_copy(cache_ref.at[layer, pt_ref[b, p], pl.ds(0, nfeat), :],
                                     buf_ref.at[slot, p], sem_ref.at[slot])

    slot = _pages_pipeline(page_copy, npages)
    eye = (lax.broadcasted_iota(jnp.int32, (page, page), 0)
           == lax.broadcasted_iota(jnp.int32, (page, page), 1)).astype(BF16)

    def to_rows(p, c):
        x = _dot_nt(eye, buf_ref[slot, p].astype(BF16))
        r0 = pl.multiple_of(p * page, page)
        rows_ref[0, pl.ds(r0, page), :] = x[:, :KV_WIDTH]
        rows_ref[1, pl.ds(r0, page), :] = x[:, KV_WIDTH:]
        return c
    lax.fori_loop(0, npages, to_rows, 0)
    nch = flat_ref.shape[1]
    _fill_flat(lambda j, s: rows_ref[s, pl.ds(j, nch, stride=CMP_STRIDE), :], flat_ref, nch)
    _compress_core(flat_ref, nch, nch - 1, pe_ref, w1_ref, b1_ref, w2_ref, kn_ref, bd_ref, out_ref)


def _compress_sample(page_table, cache_t, cw, layer):
    nb, npages = page_table.shape
    page = cache_t.shape[3]
    past = npages * page
    nch = past // CMP_STRIDE
    body = functools.partial(_compress_sample_body, layer=layer, page=page, npages=npages)
    return pl.pallas_call(
        body,
        grid_spec=pltpu.PrefetchScalarGridSpec(
            num_scalar_prefetch=1,
            grid=(nb,),
            in_specs=[pl.BlockSpec(memory_space=pl.ANY)] + _cmp_weight_specs(2),
            out_specs=pl.BlockSpec((None, 4, nch, LANES), lambda b, pt: (b, 0, 0, 0)),
            scratch_shapes=[
                pltpu.VMEM((2, npages, 2 * KV_WIDTH, page), F32),
                pltpu.VMEM((2, past, KV_WIDTH), F32),
                pltpu.VMEM((4, nch, CMP_STRIDE * HEAD_DIM), F32),
                pltpu.SemaphoreType.DMA((2,)),
            ],
        ),
        out_shape=jax.ShapeDtypeStruct((nb, 4, nch, LANES), BF16),
        compiler_params=_cparams(("arbitrary",)),
        name="compress_sample",
    )(page_table, cache_t, *cw)


def _two_part_attention(s1, mask1, v1t, s2, mask2, v2):
    s1 = jnp.where(mask1, s1, NEG)
    s2 = jnp.where(mask2, s2, NEG)
    m = jnp.maximum(jnp.max(s1, axis=-1, keepdims=True), jnp.max(s2, axis=-1, keepdims=True))
    e1 = jnp.where(mask1, jnp.exp(s1 - m), 0.0)
    e2 = jnp.where(mask2, jnp.exp(s2 - m), 0.0)
    l = jnp.sum(e1, axis=-1, keepdims=True) + jnp.sum(e2, axis=-1, keepdims=True)
    o = _dot_nt(e1.astype(BF16), v1t) + _dot(e2.astype(BF16), v2)
    return o / jnp.maximum(l, 1e-30)


def _attn_sample_body(pt_ref, cache_ref, qc_ref, qr_ref, gate_ref, cmp_ref, nsa_new_ref, win_new_ref, winc_ref,
                      pool_ref, e_ref, o_ref, buf_ref, sem_ref, *, layer, page, npages, tnew):
    nfeat = 2 * KV_WIDTH

    def page_copy(b, p, slot):
        return pltpu.make_async_copy(cache_ref.at[layer, pt_ref[b, p], pl.ds(nfeat, nfeat), :],
                                     buf_ref.at[slot, :, pl.ds(p * page, page)], sem_ref.at[slot])

    slot = _pages_pipeline(page_copy, npages)
    past = npages * page
    ncmp = cmp_ref.shape[1]
    nblk = past // SLC_BLOCK
    nlane = LANES * ((nblk + 1 + LANES - 1) // LANES)
    m4 = GROUP * tnew
    qc = qc_ref[...].astype(F32)
    qr = qr_ref[...].astype(F32)
    gates = gate_ref[...]
    nsa_new = nsa_new_ref[...]
    win_new = win_new_ref[...]
    pad = jnp.zeros((16 - tnew, HEAD_DIM), F32)
    new16 = lambda a, c0: jnp.concatenate([a[:, c0:c0 + HEAD_DIM], pad], axis=0).astype(BF16)
    t4 = lax.broadcasted_iota(jnp.int32, (m4, 1), 0) % tnew
    u16 = lax.broadcasted_iota(jnp.int32, (1, 16), 1)
    new_mask = (u16 <= t4) & (u16 < tnew)

    blk = lax.broadcasted_iota(jnp.int32, (tnew, nlane), 1)
    qpos = past + lax.broadcasted_iota(jnp.int32, (tnew, nlane), 0)
    cur = qpos // SLC_BLOCK
    forced = (blk == 0) | (blk == cur) | (blk == cur - 1)
    visible = (blk * SLC_BLOCK <= qpos) & (blk <= nblk)

    for g in range(KV_HEADS):
        hsl = slice(HEAD_DIM * g, HEAD_DIM * (g + 1))
        q4 = _stack_heads(qc, g).astype(BF16)
        s = _dot_nt(q4, cmp_ref[g][:, :HEAD_DIM])
        c_end = lax.broadcasted_iota(jnp.int32, (1, ncmp), 1) * CMP_STRIDE + (2 * CMP_STRIDE - 1)
        p = _softmax_masked(s, c_end <= past + t4)
        o_cmp = _dot(p.astype(BF16), cmp_ref[KV_HEADS + g][:, :HEAD_DIM])

        imp4 = _split_dot(p, pool_ref[...])
        imp = imp4[0:tnew]
        for n in range(1, GROUP):
            imp = imp + imp4[n * tnew:(n + 1) * tnew]
        if nlane > nblk:
            imp = jnp.concatenate([imp, jnp.zeros((tnew, nlane - nblk), F32)], axis=1)
        score = jnp.where(visible, jnp.where(forced, FORCED_SCORE, imp), -1.0)
        rank = jnp.zeros((tnew, nlane), F32)
        for k in range(nblk + 1):
            ck = score[:, k:k + 1]
            beats = (ck > score) | ((ck == score) & (blk > k))
            rank = rank + jnp.where(beats, 1.0, 0.0)
        sel = jnp.where(visible & (rank < float(SLC_TOP_N)), 1.0, 0.0)

        q4r = _stack_heads(qr, g).astype(BF16)
        ksp_t = buf_ref[slot, HEAD_DIM * g:HEAD_DIM * (g + 1), :].astype(BF16)
        vsp_t = buf_ref[slot, KV_WIDTH + HEAD_DIM * g:KV_WIDTH + HEAD_DIM * (g + 1), :].astype(BF16)
        sel4 = jnp.concatenate([sel[:, :nblk]] * GROUP, axis=0).astype(BF16)
        mpast = _dot(sel4, e_ref[...]) > 0.5
        ks_new = new16(nsa_new, 2 * KV_WIDTH + HEAD_DIM * g)
        vs_new = new16(nsa_new, 3 * KV_WIDTH + HEAD_DIM * g)
        cur_sel4 = jnp.concatenate([sel[:, nblk:nblk + 1]] * GROUP, axis=0) > 0.5
        o_slc = _two_part_attention(_dot(q4r, ksp_t), mpast, vsp_t,
                                    _dot_nt(q4r, ks_new), new_mask & cur_sel4, vs_new)

        nbuf = winc_ref.shape[1]
        kwp_t = winc_ref[HEAD_DIM * g:HEAD_DIM * (g + 1), :].astype(BF16)
        vwp_t = winc_ref[KV_WIDTH + HEAD_DIM * g:KV_WIDTH + HEAD_DIM * (g + 1), :].astype(BF16)
        kw_new = new16(win_new, HEAD_DIM * g)
        vw_new = new16(win_new, KV_WIDTH + HEAD_DIM * g)
        diff = (t4 + nbuf) - lax.broadcasted_iota(jnp.int32, (1, nbuf), 1)
        o_win = _two_part_attention(_dot(q4r, kwp_t), (diff >= 0) & (diff < WINDOW), vwp_t,
                                    _dot_nt(q4r, kw_new), new_mask, vw_new)

        for n in range(GROUP):
            hd = GROUP * g + n
            r = slice(n * tnew, (n + 1) * tnew)
            o = (o_cmp[r] * gates[:, 3 * hd:3 * hd + 1] + o_slc[r] * gates[:, 3 * hd + 1:3 * hd + 2]
                 + o_win[r] * gates[:, 3 * hd + 2:3 * hd + 3])
            o_ref[:, HEAD_DIM * hd:HEAD_DIM * (hd + 1)] = o


def _attn_sample(page_table, cache_t, qc, qr, gates, cmp, nsa_new, win_new, winc_t, pool, expand, layer, tnew):
    nb, npages = page_table.shape
    page = cache_t.shape[3]
    past = npages * page
    ncmp = cmp.shape[2]
    nblk = past // SLC_BLOCK
    nbuf = winc_t.shape[3]
    body = functools.partial(_attn_sample_body, layer=layer, page=page, npages=npages, tnew=tnew)
    per_b = lambda w: pl.BlockSpec((None, tnew, w), lambda b, pt: (b, 0, 0))
    return pl.pallas_call(
        body,
        grid_spec=pltpu.PrefetchScalarGridSpec(
            num_scalar_prefetch=1,
            grid=(nb,),
            in_specs=[
                pl.BlockSpec(memory_space=pl.ANY),
                per_b(NSA_WIDTH), per_b(NSA_WIDTH), per_b(LANES),
                pl.BlockSpec((None, 4, ncmp, LANES), lambda b, pt: (b, 0, 0, 0)),
                per_b(4 * KV_WIDTH), per_b(2 * KV_WIDTH),
                pl.BlockSpec((None, None, 2 * KV_WIDTH, nbuf), lambda b, pt: (layer, b, 0, 0)),
                pl.BlockSpec((ncmp, nblk), lambda b, pt: (0, 0)),
                pl.BlockSpec((nblk, past), lambda b, pt: (0, 0)),
            ],
            out_specs=per_b(NSA_WIDTH),
            scratch_shapes=[
                pltpu.VMEM((2, 2 * KV_WIDTH, past), F32),
                pltpu.SemaphoreType.DMA((2,)),
            ],
        ),
        out_shape=jax.ShapeDtypeStruct((nb, tnew, NSA_WIDTH), F32),
        compiler_params=_cparams(("arbitrary",)),
        name="attn_sample",
    )(page_table, cache_t, qc, qr, gates, cmp, nsa_new, win_new, winc_t, pool, expand)


def _rope_tables(pos):
    half = HEAD_DIM // 2
    inv_freq = ROPE_THETA ** (-jnp.arange(half, dtype=F32) / half)
    ang = pos.astype(F32)[:, None] * inv_freq[None, :]
    cos = jnp.cos(ang)
    sin = jnp.sin(ang)
    cos_t = jnp.tile(jnp.concatenate([cos, cos], axis=-1), (1, LANES // HEAD_DIM))
    sin_t = jnp.tile(jnp.concatenate([-sin, sin], axis=-1), (1, LANES // HEAD_DIM))
    return cos_t, sin_t


def _pack_layer(l, attn_norm_g, w_in, q_norm_g, k_norm_g, cmp_pos, cmp_w1, cmp_b1, cmp_w2,
                hg_norm_g, w_out, mlp_norm_g, w_up, w_down):
    w = w_in[l]
    gate_lo = COL_GATE
    gate_hi = COL_GATE + 3 * NSA_HEADS
    w_gate = jnp.pad(w[:, gate_lo:gate_hi], ((0, 0), (0, LANES - 3 * NSA_HEADS)))
    w_pack = jnp.concatenate([w[:, :gate_lo], w_gate, w[:, gate_hi:]], axis=1).astype(BF16)
    tile2 = lambda v: jnp.tile(v, LANES // HEAD_DIM)[None, :]
    half = CMP_STRIDE * HEAD_DIM
    w1 = cmp_w1[l]
    w1cat = jnp.concatenate([w1[:, :half], w1[:, half:]], axis=-1).astype(BF16)
    pe = cmp_pos[l].reshape(2, 2, 1, half)
    pe16 = jnp.broadcast_to(pe, (2, 2, 8, half)).reshape(2, 16, half)
    w2 = cmp_w2[l]
    w2pad = jnp.concatenate([w2, jnp.zeros_like(w2)], axis=-1).astype(BF16)
    return dict(
        g1=attn_norm_g[l][None, :], w_pack=w_pack, gq=tile2(q_norm_g[l]),
        gk=jnp.concatenate([tile2(k_norm_g[l, 1]), tile2(k_norm_g[l, 2])], axis=0),
        cw=(pe16, w1cat, cmp_b1[l][:, None, :], w2pad, tile2(k_norm_g[l, 0])),
        hg_gain=hg_norm_g[l][None, :], wo=w_out[l].astype(BF16), g2=mlp_norm_g[l][None, :],
        wu=w_up[l].astype(BF16), wd=w_down[l].astype(BF16),
    )


def _same_head_matrix():
    i = np.arange(LANES)
    return jnp.asarray((i[:, None] // HEAD_DIM) == (i[None, :] // HEAD_DIM), BF16)


def _pool_matrix(ncmp_rows, nblk):
    c = np.arange(ncmp_rows)
    j = np.arange(nblk)
    return jnp.asarray((c[:, None] // (SLC_BLOCK // CMP_STRIDE)) == j[None, :], BF16)


def _expand_matrix(nblk, nkeys):
    j = np.arange(nblk)
    s = np.arange(nkeys)
    return jnp.asarray(j[:, None] == (s[None, :] // SLC_BLOCK), BF16)


def _block_code_table(npos):
    lane = np.arange(LANES)[None, :]
    blk = (np.arange(npos) // SLC_BLOCK)[:, None]
    return jnp.asarray(np.where(lane == HEAD_DIM + blk, BLOCK_CODE, 0.0), F32)


def _feature_major(cache):
    lead = cache.shape[:-4]
    rows = cache.shape[-4]
    n = len(lead)
    perm = tuple(range(n)) + (n + 1, n + 2, n + 3, n)
    return jnp.transpose(cache, perm).reshape(*lead, -1, rows)


def kernel(x_prompt, x_sample, cache_nsa_kv, cache_win_kv, state_hgrn, page_table, attn_norm_g, w_in,
           q_norm_g, k_norm_g, cmp_pos, cmp_w1, cmp_b1, cmp_w2, hg_lb_param, hg_norm_g, w_out,
           mlp_norm_g, w_up, w_down):
    depth = w_in.shape[0]
    batch, seq, _ = x_prompt.shape
    nb, tnew, _ = x_sample.shape
    npool, page = cache_nsa_kv.shape[1], cache_nsa_kv.shape[2]
    npages = page_table.shape[1]
    past = npages * page
    nbuf = cache_win_kv.shape[2]

    lb_all = jnp.cumsum(jax.nn.softmax(hg_lb_param.astype(F32), axis=0), axis=0)
    lb_all = lb_all - lb_all[0]

    cos_p, sin_p = _rope_tables(jnp.arange(seq, dtype=jnp.int32))
    cos_s, sin_s = _rope_tables(past + jnp.arange(tnew, dtype=jnp.int32))
    cos_s = jnp.tile(cos_s, (nb, 1))
    sin_s = jnp.tile(sin_s, (nb, 1))
    bd = _same_head_matrix()
    nsel_p = seq // SLC_BLOCK
    poolt_p = _pool_matrix(seq // CMP_STRIDE, nsel_p).T
    code_p = _block_code_table(seq)
    code_s = jnp.zeros((nb * tnew, LANES), F32)
    pool_s = _pool_matrix(past // CMP_STRIDE, past // SLC_BLOCK)
    expand_s = _expand_matrix(past // SLC_BLOCK, past)

    cache_t = _feature_major(cache_nsa_kv)
    winc_t = _feature_major(cache_win_kv)
    zero_state = jnp.zeros((batch, HG_HEADS, HG_DK, HG_DK), F32)

    tq = 128
    hg_tile = 128
    xp = x_prompt.reshape(batch * seq, D_MODEL)
    xs = x_sample.reshape(nb * tnew, D_MODEL)
    p_nsa, p_win, p_hg, s_nsa, s_win, s_hg = [], [], [], [], [], []
    for l in range(depth):
        pk = _pack_layer(l, attn_norm_g, w_in, q_norm_g, k_norm_g, cmp_pos, cmp_w1, cmp_b1, cmp_w2,
                         hg_norm_g, w_out, mlp_norm_g, w_up, w_down)
        lb = lb_all[l][None, :]

        qc, qr, nsa, win, kva, gates, hgin = _proj(xp, pk["g1"], pk["w_pack"], cos_p, sin_p, code_p, pk["gq"], pk["gk"], bd, 512)
        cmp = _compress_prompt(nsa, batch, seq, pk["cw"] + (bd,))
        o_nsa = _attn_prompt(qc, qr, gates, cmp, kva, poolt_p, batch, seq, tq, 512)
        o_hg, hfin = _hgrn(hgin, lb, pk["hg_gain"], zero_state, batch, seq, hg_tile, hg_tile)
        xp = _mlp(xp, o_nsa, o_hg, pk["wo"], pk["g2"], pk["wu"], pk["wd"], 512)
        p_nsa.append(nsa.reshape(batch, seq, 4, KV_HEADS, HEAD_DIM))
        p_win.append(win.reshape(batch, seq, 2, KV_HEADS, HEAD_DIM)[:, seq - min(WINDOW, seq):])
        p_hg.append(hfin)

        ns = nb * tnew
        qc, qr, nsa, win, _, gates, hgin = _proj(xs, pk["g1"], pk["w_pack"], cos_s, sin_s, code_s, pk["gq"], pk["gk"], bd, ns)
        cmp = _compress_sample(page_table, cache_t, pk["cw"] + (bd,), l)
        r3 = lambda a: a.reshape(nb, tnew, a.shape[-1])
        o_nsa = _attn_sample(page_table, cache_t, r3(qc), r3(qr), r3(gates), cmp, r3(nsa), r3(win), winc_t,
                             pool_s, expand_s, l, tnew)
        hg_pad = jnp.pad(r3(hgin), ((0, 0), (0, hg_tile - tnew), (0, 0))).reshape(nb * hg_tile, 4 * HG_WIDTH)
        o_hg, hfin = _hgrn(hg_pad, lb, pk["hg_gain"], state_hgrn[l], nb, hg_tile, hg_tile, tnew)
        o_hg = o_hg.reshape(nb, hg_tile, HG_WIDTH)[:, :tnew].reshape(ns, HG_WIDTH)
        xs = _mlp(xs, o_nsa.reshape(ns, NSA_WIDTH).astype(BF16), o_hg, pk["wo"], pk["g2"], pk["wu"], pk["wd"], ns)
        s_nsa.append(nsa.reshape(nb, tnew, 4, KV_HEADS, HEAD_DIM))
        s_win.append(win.reshape(nb, tnew, 2, KV_HEADS, HEAD_DIM))
        s_hg.append(hfin)

    return (xp.reshape(batch, seq, D_MODEL), xs.reshape(nb, tnew, D_MODEL),
            jnp.stack(p_nsa), jnp.stack(p_win), jnp.stack(p_hg),
            jnp.stack(s_nsa), jnp.stack(s_win), jnp.stack(s_hg))
```

```python
import functools

import numpy as np
import jax
import jax.numpy as jnp
from jax import lax
from jax.experimental import pallas as pl
from jax.experimental.pallas import tpu as pltpu

F32 = jnp.float32
BF16 = jnp.bfloat16

D_MODEL = 1024
HEAD_DIM = 64
NSA_HEADS = 8
KV_HEADS = 2
GROUP = NSA_HEADS // KV_HEADS
NSA_WIDTH = NSA_HEADS * HEAD_DIM
KV_WIDTH = KV_HEADS * HEAD_DIM
CMP_STRIDE = 16
CMP_HIDDEN = 4 * HEAD_DIM
SLC_BLOCK = 64
SLC_TOP_N = 16
WINDOW = 512
FORCED_SCORE = 1.0e4
HG_HEADS = 4
HG_DK = 128
HG_WIDTH = HG_HEADS * HG_DK
HG_CHUNK = 32
D_FF = 4 * D_MODEL
ROPE_THETA = 10000.0
EPS = 1e-6
SCALE = HEAD_DIM ** -0.5
NEG = -1e30
BLOCK_CODE = -(2.0 ** 100)

LANES = 128
PA_W = NSA_WIDTH + 6 * KV_WIDTH + LANES
COL_KC, COL_VC, COL_KS, COL_VS, COL_KW, COL_VW = (NSA_WIDTH + i * KV_WIDTH for i in range(6))
COL_GATE = NSA_WIDTH + 6 * KV_WIDTH
VMEM_LIMIT = 56 * 1024 * 1024


def _cparams(sem):
    return pltpu.CompilerParams(dimension_semantics=sem, vmem_limit_bytes=VMEM_LIMIT)


def _dot(a, b):
    return jnp.dot(a, b, preferred_element_type=F32)


def _dot_nt(a, b):
    return lax.dot_general(a, b, (((1,), (1,)), ((), ())), preferred_element_type=F32)


def _split_dot(a, b, nt=False):
    hi = a.astype(BF16)
    lo = (a - hi.astype(F32)).astype(BF16)
    f = _dot_nt if nt else _dot
    return f(hi, b) + f(lo, b)


def _head_norm(v, bd, gain):
    ss = _split_dot(v * v, bd)
    return v * lax.rsqrt(ss * (1.0 / HEAD_DIM) + EPS) * gain


def _rope(v, cos, sin, first_half):
    sw = jnp.where(first_half, pltpu.roll(v, LANES - HEAD_DIM // 2, 1), pltpu.roll(v, HEAD_DIM // 2, 1))
    return v * cos + sw * sin


def _sigmoid(x):
    return 1.0 / (1.0 + jnp.exp(-x))


def _proj_body(x_ref, g_ref, w_ref, cos_ref, sin_ref, code_ref, gq_ref, gk_ref, bd_ref,
               qc_ref, qr_ref, nsa_ref, win_ref, kva_ref, gate_ref, hgin_ref):
    x = x_ref[...]
    ms = jnp.mean(x * x, axis=-1, keepdims=True)
    h = (x * lax.rsqrt(ms + EPS) * g_ref[...]).astype(BF16)
    pa = _dot(h, w_ref[:, :PA_W])
    hgin_ref[...] = _dot(h, w_ref[:, PA_W:])
    cos = cos_ref[...]
    sin = sin_ref[...]
    bd = bd_ref[...]
    lane = lax.broadcasted_iota(jnp.int32, cos.shape, 1)
    first_half = (lane % HEAD_DIM) < (HEAD_DIM // 2)
    gq = gq_ref[...]
    for c in range(NSA_WIDTH // LANES):
        sl = slice(LANES * c, LANES * (c + 1))
        qn = _head_norm(pa[:, sl], bd, gq) * SCALE
        qc_ref[:, sl] = qn.astype(BF16)
        qr_ref[:, sl] = _rope(qn, cos, sin, first_half).astype(BF16)
    kc = pa[:, COL_KC:COL_KC + KV_WIDTH]
    vc = pa[:, COL_VC:COL_VC + KV_WIDTH]
    ks = _rope(_head_norm(pa[:, COL_KS:COL_KS + KV_WIDTH], bd, gk_ref[0:1, :]), cos, sin, first_half)
    vs = pa[:, COL_VS:COL_VS + KV_WIDTH]
    kw = _rope(_head_norm(pa[:, COL_KW:COL_KW + KV_WIDTH], bd, gk_ref[1:2, :]), cos, sin, first_half)
    vw = pa[:, COL_VW:COL_VW + KV_WIDTH]
    for i, a in enumerate((kc, vc, ks, vs)):
        nsa_ref[:, LANES * i:LANES * (i + 1)] = a
    win_ref[:, 0:LANES] = kw
    win_ref[:, LANES:2 * LANES] = vw
    low = lane < HEAD_DIM
    code = code_ref[...]
    aug = []
    for kk, vv, fill in ((ks, vs, code), (kw, vw, 0.0)):
        aug += [jnp.where(low, kk, fill), jnp.where(low, pltpu.roll(kk, HEAD_DIM, 1), fill),
                jnp.where(low, vv, 1.0), jnp.where(low, pltpu.roll(vv, HEAD_DIM, 1), 1.0)]
    for i, a in enumerate(aug):
        kva_ref[:, LANES * i:LANES * (i + 1)] = a.astype(BF16)
    gate_ref[...] = _sigmoid(pa[:, COL_GATE:COL_GATE + LANES])


def _proj(x, g, w, cos, sin, code, gq, gk, bd, tn):
    n = x.shape[0]
    nt = cos.shape[0] // tn
    wtot = w.shape[1]
    row = lambda i: (i, 0)
    const = lambda i: (0, 0)
    outs = [
        jax.ShapeDtypeStruct((n, NSA_WIDTH), BF16),
        jax.ShapeDtypeStruct((n, NSA_WIDTH), BF16),
        jax.ShapeDtypeStruct((n, 4 * KV_WIDTH), F32),
        jax.ShapeDtypeStruct((n, 2 * KV_WIDTH), F32),
        jax.ShapeDtypeStruct((n, 8 * LANES), BF16),
        jax.ShapeDtypeStruct((n, LANES), F32),
        jax.ShapeDtypeStruct((n, 4 * HG_WIDTH), F32),
    ]
    return pl.pallas_call(
        _proj_body,
        grid=(n // tn,),
        in_specs=[
            pl.BlockSpec((tn, D_MODEL), row),
            pl.BlockSpec((1, D_MODEL), const),
            pl.BlockSpec((D_MODEL, wtot), const),
            pl.BlockSpec((tn, LANES), lambda i: (i % nt, 0)),
            pl.BlockSpec((tn, LANES), lambda i: (i % nt, 0)),
            pl.BlockSpec((tn, LANES), lambda i: (i % nt, 0)),
            pl.BlockSpec((1, LANES), const),
            pl.BlockSpec((2, LANES), const),
            pl.BlockSpec((LANES, LANES), const),
        ],
        out_specs=[pl.BlockSpec((tn, o.shape[1]), row) for o in outs],
        out_shape=outs,
        compiler_params=_cparams(("parallel",)),
        name="proj",
    )(x, g, w, cos, sin, code, gq, gk, bd)


def _gelu_tanh(x):
    return 0.5 * x * (1.0 + jnp.tanh(0.7978845608028654 * (x + 0.044715 * (x * x * x))))


def _fill_flat(load_rows, flat_ref, nch):
    for j in range(CMP_STRIDE):
        for slot in range(2):
            rows = load_rows(j, slot)
            for g in range(KV_HEADS):
                flat_ref[2 * slot + g, :, HEAD_DIM * j:HEAD_DIM * (j + 1)] = rows[:, HEAD_DIM * g:HEAD_DIM * (g + 1)]


def _compress_core(flat_ref, nch, nvalid, pe_ref, w1_ref, b1_ref, w2_ref, kn_ref, bd_ref, out_ref):
    rowid = lax.broadcasted_iota(jnp.int32, (nch, LANES), 0)
    low = lax.broadcasted_iota(jnp.int32, (nch, LANES), 1) < HEAD_DIM
    for slot in range(2):
        w1 = w1_ref[slot]
        pe = pe_ref[slot]
        pw = _split_dot(pe, w1)
        const = pw[0:1, :CMP_HIDDEN] + pw[8:9, CMP_HIDDEN:] + b1_ref[slot]
        for g in range(KV_HEADS):
            f = flat_ref[2 * slot + g].astype(BF16)
            ab = _dot(f, w1)
            pre = ab[:, :CMP_HIDDEN] + pltpu.roll(ab[:, CMP_HIDDEN:], nch - 1, 0) + const
            hid = _gelu_tanh(pre).astype(BF16)
            o = _dot(hid, w2_ref[slot])
            if slot == 0:
                o = _head_norm(o, bd_ref[...], kn_ref[...])
            else:
                o = jnp.where(low, o, 1.0)
            out_ref[2 * slot + g] = jnp.where(rowid < nvalid, o, 0.0).astype(BF16)


def _compress_prompt_body(xk_ref, xv_ref, pe_ref, w1_ref, b1_ref, w2_ref, kn_ref, bd_ref, out_ref, flat_ref):
    nch = flat_ref.shape[1]
    _fill_flat(lambda j, slot: (xk_ref, xv_ref)[slot][pl.ds(j, nch, stride=CMP_STRIDE), :], flat_ref, nch)
    _compress_core(flat_ref, nch, nch - 1, pe_ref, w1_ref, b1_ref, w2_ref, kn_ref, bd_ref, out_ref)


def _cmp_weight_specs(nidx):
    z = lambda *a: (0,) * nidx
    return [
        pl.BlockSpec((2, 16, CMP_STRIDE * HEAD_DIM), lambda *a: (0, 0, 0)),
        pl.BlockSpec((2, CMP_STRIDE * HEAD_DIM, 2 * CMP_HIDDEN), lambda *a: (0, 0, 0)),
        pl.BlockSpec((2, 1, CMP_HIDDEN), lambda *a: (0, 0, 0)),
        pl.BlockSpec((2, CMP_HIDDEN, LANES), lambda *a: (0, 0, 0)),
        pl.BlockSpec((1, LANES), lambda *a: (0, 0)),
        pl.BlockSpec((LANES, LANES), lambda *a: (0, 0)),
    ]


def _compress_prompt(nsa, batch, seq, cw):
    nch = seq // CMP_STRIDE
    return pl.pallas_call(
        _compress_prompt_body,
        grid=(batch,),
        in_specs=[pl.BlockSpec((seq, KV_WIDTH), lambda b: (b, 0)),
                  pl.BlockSpec((seq, KV_WIDTH), lambda b: (b, 1))] + _cmp_weight_specs(1),
        out_specs=pl.BlockSpec((None, 4, nch, LANES), lambda b: (b, 0, 0, 0)),
        out_shape=jax.ShapeDtypeStruct((batch, 4, nch, LANES), BF16),
        scratch_shapes=[pltpu.VMEM((4, nch, CMP_STRIDE * HEAD_DIM), F32)],
        compiler_params=_cparams(("parallel",)),
        name="compress_prompt",
    )(nsa, nsa, *cw)


def _softmax_masked(s, mask):
    sm = jnp.where(mask, s, NEG)
    m = jnp.max(sm, axis=-1, keepdims=True)
    e = jnp.where(mask, jnp.exp(sm - m), 0.0)
    return e / jnp.maximum(jnp.sum(e, axis=-1, keepdims=True), 1e-30)


def _stack_heads(q, g, rows_f32=False):
    parts = [q[:, HEAD_DIM * (GROUP * g + n):HEAD_DIM * (GROUP * g + n + 1)] for n in range(GROUP)]
    return jnp.concatenate(parts, axis=0)


def _place_heads(q, g, fill, low):
    parts = []
    for n in range(GROUP):
        hd = GROUP * g + n
        chunk = q[:, LANES * (hd // 2):LANES * (hd // 2 + 1)].astype(F32)
        if hd % 2:
            chunk = pltpu.roll(chunk, HEAD_DIM, 1)
        parts.append(jnp.where(low, chunk, fill).astype(BF16))
    return jnp.concatenate(parts, axis=0)


def _attend(q4, k, v, bias):
    s = _add_bias(_dot_nt(q4, k), bias)
    e = jnp.exp(s - jnp.max(s, axis=-1, keepdims=True))
    return _dot(e.astype(BF16), v), e


def _add_bias(s, bias):
    if bias is None:
        return s
    tq, width = bias.shape
    return (s.reshape(GROUP, tq, width) + bias[None]).reshape(GROUP * tq, width)


def _attn_prompt_body(qc_ref, qr_ref, gate_ref, cmp_ref, slc_ref, win_ref, poolt_ref, o_ref, score_ref,
                      *, tq, seq, kchunk):
    i = pl.program_id(1)
    q0 = i * tq
    ncmp = cmp_ref.shape[1]
    nsel = seq // SLC_BLOCK
    m4 = GROUP * tq
    qc = qc_ref[...]
    qr = qr_ref[...]
    gates = gate_ref[...]
    low = lax.broadcasted_iota(jnp.int32, (tq, LANES), 1) < HEAD_DIM
    qpos = q0 + lax.broadcasted_iota(jnp.int32, (tq, 1), 0)
    rep4 = lambda a: jnp.concatenate([a] * GROUP, axis=0)

    c_end = lax.broadcasted_iota(jnp.int32, (1, ncmp), 1) * CMP_STRIDE + (2 * CMP_STRIDE - 1)
    cmp_bias = jnp.where(c_end <= qpos, 0.0, NEG)
    cmp_any = rep4(jnp.where(qpos >= 2 * CMP_STRIDE - 1, 1.0, 0.0))
    wlen = WINDOW + tq
    w0 = pl.multiple_of(jnp.maximum(q0 - WINDOW, 0), tq)
    diff = qpos - (w0 + lax.broadcasted_iota(jnp.int32, (1, wlen), 1))
    win_bias = jnp.where((diff >= 0) & (diff < WINDOW), 0.0, NEG)
    nk = (q0 + tq + kchunk - 1) // kchunk
    kpos_d = (nk - 1) * kchunk + lax.broadcasted_iota(jnp.int32, (1, kchunk), 1)
    diag_bias = jnp.where(kpos_d <= qpos, 0.0, NEG)

    blk = lax.broadcasted_iota(jnp.int32, (nsel, tq), 0)
    qp_l = q0 + lax.broadcasted_iota(jnp.int32, (nsel, tq), 1)
    cur = qp_l // SLC_BLOCK
    forced = (blk == 0) | (blk == cur) | (blk == cur - 1)
    visible = blk * SLC_BLOCK <= qp_l
    nvis = jnp.minimum((q0 + tq + SLC_BLOCK - 1) // SLC_BLOCK, nsel)

    kcol = lambda g: slice(LANES * g, LANES * (g + 1))
    vcol = lambda g: slice(LANES * (KV_HEADS + g), LANES * (KV_HEADS + g + 1))

    for g in range(KV_HEADS):
        acc_c, e_c = _attend(_place_heads(qc, g, 0.0, low), cmp_ref[g], cmp_ref[KV_HEADS + g], cmp_bias)
        inv_c = cmp_any / acc_c[:, HEAD_DIM:HEAD_DIM + 1]
        o_cmp = acc_c * inv_c
        p = e_c * inv_c
        psum = p[0:tq] + p[tq:2 * tq] + p[2 * tq:3 * tq] + p[3 * tq:4 * tq]

        p_hi = psum.astype(BF16)
        p_lo = (psum - p_hi.astype(F32)).astype(BF16)
        imp_t = _dot_nt(poolt_ref[...], p_hi) + _dot_nt(poolt_ref[...], p_lo)
        score = jnp.where(visible, jnp.where(forced, FORCED_SCORE, imp_t), -1.0)
        score_ref[...] = score

        def rank_step(k, rank):
            rk = score_ref[pl.ds(k, 1), :]
            return rank + jnp.where((rk > score) | ((rk == score) & (blk > k)), 1.0, 0.0)

        rank = lax.fori_loop(0, nvis, rank_step, jnp.zeros((nsel, tq), F32))
        notsel = jnp.where(visible & (rank < float(SLC_TOP_N)), 0.0, 1.0)
        pieces = [jnp.zeros((HEAD_DIM, tq), F32), notsel]
        if nsel < HEAD_DIM:
            pieces.append(jnp.ones((HEAD_DIM - nsel, tq), F32))
        fill = jnp.concatenate(pieces, axis=0).T

        q4r = _place_heads(qr, g, fill, low)

        def absorb(c, m_i, acc, bias):
            k0 = pl.multiple_of(c * kchunk, kchunk)
            s = _add_bias(_dot_nt(q4r, slc_ref[pl.ds(k0, kchunk), kcol(g)]), bias)
            m_new = jnp.maximum(m_i, jnp.max(s, axis=-1, keepdims=True))
            e = jnp.exp(s - m_new).astype(BF16)
            return m_new, jnp.exp(m_i - m_new) * acc + _dot(e, slc_ref[pl.ds(k0, kchunk), vcol(g)])

        st = (jnp.full((m4, 1), NEG, F32), jnp.zeros((m4, LANES), F32))
        m_l, acc_l = lax.fori_loop(0, nk - 1, lambda c, cr: absorb(c, cr[0], cr[1], None), st)
        _, acc_s = absorb(nk - 1, m_l, acc_l, diag_bias)
        o_slc = acc_s / acc_s[:, HEAD_DIM:HEAD_DIM + 1]

        acc_w, _ = _attend(q4r, win_ref[pl.ds(w0, wlen), kcol(g)], win_ref[pl.ds(w0, wlen), vcol(g)], win_bias)
        o_win = acc_w / acc_w[:, HEAD_DIM:HEAD_DIM + 1]

        comb = []
        for n in range(GROUP):
            hd = GROUP * g + n
            r = slice(n * tq, (n + 1) * tq)
            comb.append(o_cmp[r] * gates[:, 3 * hd:3 * hd + 1] + o_slc[r] * gates[:, 3 * hd + 1:3 * hd + 2]
                        + o_win[r] * gates[:, 3 * hd + 2:3 * hd + 3])
        for pair in range(GROUP // 2):
            both = jnp.where(low, comb[2 * pair], pltpu.roll(comb[2 * pair + 1], HEAD_DIM, 1))
            c0 = LANES * ((GROUP // 2) * g + pair)
            o_ref[:, c0:c0 + LANES] = both.astype(BF16)


def _attn_prompt(qc, qr, gates, cmp, kva, poolt, batch, seq, tq, kchunk):
    n = batch * seq
    nq = seq // tq
    ncmp = cmp.shape[2]
    nsel = seq // SLC_BLOCK
    assert nsel <= HEAD_DIM, "block codes use the 64 spare contraction lanes"
    body = functools.partial(_attn_prompt_body, tq=tq, seq=seq, kchunk=kchunk)
    return pl.pallas_call(
        body,
        grid=(batch, nq),
        in_specs=[
            pl.BlockSpec((tq, NSA_WIDTH), lambda b, i: (b * nq + i, 0)),
            pl.BlockSpec((tq, NSA_WIDTH), lambda b, i: (b * nq + i, 0)),
            pl.BlockSpec((tq, LANES), lambda b, i: (b * nq + i, 0)),
            pl.BlockSpec((None, 4, ncmp, LANES), lambda b, i: (b, 0, 0, 0)),
            pl.BlockSpec((seq, 4 * LANES), lambda b, i: (b, 0)),
            pl.BlockSpec((seq, 4 * LANES), lambda b, i: (b, 1)),
            pl.BlockSpec((nsel, ncmp), lambda b, i: (0, 0)),
        ],
        out_specs=pl.BlockSpec((tq, NSA_WIDTH), lambda b, i: (b * nq + i, 0)),
        out_shape=jax.ShapeDtypeStruct((n, NSA_WIDTH), BF16),
        scratch_shapes=[pltpu.VMEM((nsel, tq), F32)],
        compiler_params=_cparams(("parallel", "parallel")),
        name="attn_prompt",
    )(qc, qr, gates, cmp, kva, kva, poolt)


def _hgrn_body(x_ref, lb_ref, gain_ref, s0_ref, o_ref, sfin_ref, st_ref, *, tile, nvalid):
    i = pl.program_id(1)
    nsub = tile // HG_CHUNK

    @pl.when(i == 0)
    def _():
        for hd in range(HG_HEADS):
            st_ref[hd] = s0_ref[hd].T

    row = lax.broadcasted_iota(jnp.int32, (tile, HG_DK), 0)
    rowc = row % HG_CHUNK
    col = lax.broadcasted_iota(jnp.int32, (tile, tile), 1)
    rr = lax.broadcasted_iota(jnp.int32, (tile, tile), 0)
    intra_mask = (rr // HG_CHUNK == col // HG_CHUNK) & (rr >= col)
    lane_t = lax.broadcasted_iota(jnp.int32, (HG_DK, tile), 1)
    valid = row < nvalid

    for hd in range(HG_HEADS):
        sl = lambda k: slice(k * HG_WIDTH + hd * HG_DK, k * HG_WIDTH + (hd + 1) * HG_DK)
        q = x_ref[:, sl(0)]
        z = x_ref[:, sl(1)]
        v = x_ref[:, sl(2)]
        gt = x_ref[:, sl(3)]
        lb = lb_ref[:, hd * HG_DK:(hd + 1) * HG_DK]
        log_sig = jnp.minimum(z, 0.0) - jnp.log(1.0 + jnp.exp(-jnp.abs(z)))
        t1 = jnp.log(1.0 - lb) + log_sig
        a = jnp.broadcast_to(jnp.log(lb), t1.shape)
        log_f = jnp.maximum(a, t1) + jnp.log(1.0 + jnp.exp(-jnp.abs(a - t1)))
        key = (1.0 - lb) * _sigmoid(-z)
        if nvalid < tile:
            log_f = jnp.where(valid, log_f, 0.0)
            key = jnp.where(valid, key, 0.0)
            v = jnp.where(valid, v, 0.0)
        qs = q * _sigmoid(q)
        b = log_f
        step = 1
        while step < HG_CHUNK:
            b = b + jnp.where(rowc >= step, pltpu.roll(b, step, 0), 0.0)
            step *= 2
        qe = (qs * jnp.exp(b)).astype(BF16)
        ke = (key * jnp.exp(-b)).astype(BF16)
        vb = v.astype(BF16)
        a_mat = jnp.where(intra_mask, _dot_nt(qe, ke), 0.0)
        o_acc = _dot(a_mat.astype(BF16), vb)
        vt = v.T
        b_lasts = [b[(c + 1) * HG_CHUNK - 1:(c + 1) * HG_CHUNK, :] for c in range(nsub)]
        b_end = jnp.broadcast_to(b_lasts[0], b.shape)
        for c in range(1, nsub):
            b_end = jnp.where(row >= c * HG_CHUNK, b_lasts[c], b_end)
        kd = (key * jnp.exp(b_end - b)).astype(BF16)
        st = st_ref[hd]
        o_parts = []
        for c in range(nsub):
            r = slice(c * HG_CHUNK, (c + 1) * HG_CHUNK)
            o_parts.append(_dot_nt(qe[r], st.astype(BF16)))
            vtc = jnp.where(lane_t // HG_CHUNK == c, vt, 0.0).astype(BF16)
            st = st * jnp.exp(b_lasts[c]) + _dot(vtc, kd)
        st_ref[hd] = st
        o = o_acc + jnp.concatenate(o_parts, axis=0)
        ms = jnp.mean(o * o, axis=-1, keepdims=True)
        o = o * lax.rsqrt(ms + EPS) * gain_ref[...] * _sigmoid(gt)
        o_ref[:, hd * HG_DK:(hd + 1) * HG_DK] = o.astype(BF16)

    @pl.when(i == pl.num_programs(1) - 1)
    def _():
        for hd in range(HG_HEADS):
            sfin_ref[hd] = st_ref[hd].T


def _hgrn(hgin, lb, gain, s0, batch, seq, tile, nvalid):
    nt = seq // tile
    body = functools.partial(_hgrn_body, tile=tile, nvalid=nvalid)
    return pl.pallas_call(
        body,
        grid=(batch, nt),
        in_specs=[
            pl.BlockSpec((tile, 4 * HG_WIDTH), lambda b, i: (b * nt + i, 0)),
            pl.BlockSpec((1, HG_WIDTH), lambda b, i: (0, 0)),
            pl.BlockSpec((1, HG_DK), lambda b, i: (0, 0)),
            pl.BlockSpec((None, HG_HEADS, HG_DK, HG_DK), lambda b, i: (b, 0, 0, 0)),
        ],
        out_specs=[
            pl.BlockSpec((tile, HG_WIDTH), lambda b, i: (b * nt + i, 0)),
            pl.BlockSpec((None, HG_HEADS, HG_DK, HG_DK), lambda b, i: (b, 0, 0, 0)),
        ],
        out_shape=[
            jax.ShapeDtypeStruct((batch * seq, HG_WIDTH), BF16),
            jax.ShapeDtypeStruct((batch, HG_HEADS, HG_DK, HG_DK), F32),
        ],
        scratch_shapes=[pltpu.VMEM((HG_HEADS, HG_DK, HG_DK), F32)],
        compiler_params=_cparams(("parallel", "arbitrary")),
        name="hgrn",
    )(hgin, lb, gain, s0)


def _mlp_body(x_ref, on_ref, oh_ref, wo_ref, g_ref, wu_ref, wd_ref, y_ref, *, fchunk):
    x2 = x_ref[...] + _dot(jnp.concatenate([on_ref[...], oh_ref[...]], axis=1), wo_ref[...])
    ms = jnp.mean(x2 * x2, axis=-1, keepdims=True)
    h2 = (x2 * lax.rsqrt(ms + EPS) * g_ref[...]).astype(BF16)
    acc = x2
    for c in range(D_FF // fchunk):
        u = _dot(h2, wu_ref[:, c * fchunk:(c + 1) * fchunk])
        u = jnp.maximum(u, 0.0)
        acc = acc + _dot((u * u).astype(BF16), wd_ref[c * fchunk:(c + 1) * fchunk, :])
    y_ref[...] = acc


def _mlp(x, on, oh, wo, g, wu, wd, layer, tn, fchunk=1024):
    n = x.shape[0]
    row = lambda i: (i, 0)
    const = lambda i: (0, 0)
    slab = lambda i: (layer, 0, 0)
    return pl.pallas_call(
        functools.partial(_mlp_body, fchunk=fchunk),
        grid=(n // tn,),
        in_specs=[
            pl.BlockSpec((tn, D_MODEL), row),
            pl.BlockSpec((tn, NSA_WIDTH), row),
            pl.BlockSpec((tn, HG_WIDTH), row),
            pl.BlockSpec((None, D_MODEL, D_MODEL), slab),
            pl.BlockSpec((1, D_MODEL), const),
            pl.BlockSpec((None, D_MODEL, D_FF), slab),
            pl.BlockSpec((None, D_FF, D_MODEL), slab),
        ],
        out_specs=pl.BlockSpec((tn, D_MODEL), row),
        out_shape=jax.ShapeDtypeStruct((n, D_MODEL), F32),
        compiler_params=_cparams(("parallel",)),
        name="mlp",
    )(x, on, oh, wo, g, wu, wd)


def _pages_pipeline(page_copy, npages):
    b = pl.program_id(0)
    nb = pl.num_programs(0)
    slot = b % 2

    def start(bb, sl):
        def go(p, c):
            page_copy(bb, p, sl).start()
            return c
        lax.fori_loop(0, npages, go, 0)

    @pl.when(b == 0)
    def _():
        start(0, 0)

    @pl.when(b + 1 < nb)
    def _():
        start(b + 1, 1 - slot)

    def wait(p, c):
        page_copy(b, p, slot).wait()
        return c
    lax.fori_loop(0, npages, wait, 0)
    return slot


def _compress_sample_body(pt_ref, cache_ref, pe_ref, w1_ref, b1_ref, w2_ref, kn_ref, bd_ref,
                          out_ref, buf_ref, rows_ref, flat_ref, sem_ref, *, layer, page, npages):
    nfeat = 2 * KV_WIDTH

    def page_copy(b, p, slot):
        return pltpu.make_async_copy(cache_ref.at[layer, pt_ref[b, p], pl.ds(0, nfeat), :],
                                     buf_ref.at[slot, p], sem_ref.at[slot])

    slot = _pages_pipeline(page_copy, npages)
    eye = (lax.broadcasted_iota(jnp.int32, (page, page), 0)
           == lax.broadcasted_iota(jnp.int32, (page, page), 1)).astype(BF16)

    def to_rows(p, c):
        x = _dot_nt(eye, buf_ref[slot, p].astype(BF16))
        r0 = pl.multiple_of(p * page, page)
        rows_ref[0, pl.ds(r0, page), :] = x[:, :KV_WIDTH]
        rows_ref[1, pl.ds(r0, page), :] = x[:, KV_WIDTH:]
        return c
    lax.fori_loop(0, npages, to_rows, 0, unroll=8)
    nch = flat_ref.shape[1]
    _fill_flat(lambda j, s: rows_ref[s, pl.ds(j, nch, stride=CMP_STRIDE), :], flat_ref, nch)
    _compress_core(flat_ref, nch, nch - 1, pe_ref, w1_ref, b1_ref, w2_ref, kn_ref, bd_ref, out_ref)


def _compress_sample(page_table, cache_t, cw, layer):
    nb, npages = page_table.shape
    page = cache_t.shape[3]
    past = npages * page
    nch = past // CMP_STRIDE
    body = functools.partial(_compress_sample_body, layer=layer, page=page, npages=npages)
    return pl.pallas_call(
        body,
        grid_spec=pltpu.PrefetchScalarGridSpec(
            num_scalar_prefetch=1,
            grid=(nb,),
            in_specs=[pl.BlockSpec(memory_space=pl.ANY)] + _cmp_weight_specs(2),
            out_specs=pl.BlockSpec((None, 4, nch, LANES), lambda b, pt: (b, 0, 0, 0)),
            scratch_shapes=[
                pltpu.VMEM((2, npages, 2 * KV_WIDTH, page), F32),
                pltpu.VMEM((2, past, KV_WIDTH), F32),
                pltpu.VMEM((4, nch, CMP_STRIDE * HEAD_DIM), F32),
                pltpu.SemaphoreType.DMA((2,)),
            ],
        ),
        out_shape=jax.ShapeDtypeStruct((nb, 4, nch, LANES), BF16),
        compiler_params=_cparams(("arbitrary",)),
        name="compress_sample",
    )(page_table, cache_t, *cw)


def _two_part_attention(s1, mask1, v1t, s2, mask2, v2):
    s1 = jnp.where(mask1, s1, NEG)
    s2 = jnp.where(mask2, s2, NEG)
    m = jnp.maximum(jnp.max(s1, axis=-1, keepdims=True), jnp.max(s2, axis=-1, keepdims=True))
    e1 = jnp.where(mask1, jnp.exp(s1 - m), 0.0)
    e2 = jnp.where(mask2, jnp.exp(s2 - m), 0.0)
    l = jnp.sum(e1, axis=-1, keepdims=True) + jnp.sum(e2, axis=-1, keepdims=True)
    o = _dot_nt(e1.astype(BF16), v1t) + _dot(e2.astype(BF16), v2)
    return o / jnp.maximum(l, 1e-30)


def _attn_sample_body(pt_ref, cache_ref, qc_ref, qr_ref, gate_ref, cmp_ref, nsa_new_ref, win_new_ref, winc_ref,
                      pool_ref, e_ref, o_ref, buf_ref, sem_ref, *, layer, page, npages, tnew):
    nfeat = 2 * KV_WIDTH

    def page_copy(b, p, slot):
        return pltpu.make_async_copy(cache_ref.at[layer, pt_ref[b, p], pl.ds(nfeat, nfeat), :],
                                     buf_ref.at[slot, :, pl.ds(p * page, page)], sem_ref.at[slot])

    slot = _pages_pipeline(page_copy, npages)
    past = npages * page
    ncmp = cmp_ref.shape[1]
    nblk = past // SLC_BLOCK
    nlane = LANES * ((nblk + 1 + LANES - 1) // LANES)
    m4 = GROUP * tnew
    qc = qc_ref[...].astype(F32)
    qr = qr_ref[...].astype(F32)
    gates = gate_ref[...]
    nsa_new = nsa_new_ref[...]
    win_new = win_new_ref[...]
    pad = jnp.zeros((16 - tnew, HEAD_DIM), F32)
    new16 = lambda a, c0: jnp.concatenate([a[:, c0:c0 + HEAD_DIM], pad], axis=0).astype(BF16)
    t4 = lax.broadcasted_iota(jnp.int32, (m4, 1), 0) % tnew
    u16 = lax.broadcasted_iota(jnp.int32, (1, 16), 1)
    new_mask = (u16 <= t4) & (u16 < tnew)

    blk = lax.broadcasted_iota(jnp.int32, (tnew, nlane), 1)
    qpos = past + lax.broadcasted_iota(jnp.int32, (tnew, nlane), 0)
    cur = qpos // SLC_BLOCK
    forced = (blk == 0) | (blk == cur) | (blk == cur - 1)
    visible = (blk * SLC_BLOCK <= qpos) & (blk <= nblk)

    for g in range(KV_HEADS):
        hsl = slice(HEAD_DIM * g, HEAD_DIM * (g + 1))
        q4 = _stack_heads(qc, g).astype(BF16)
        s = _dot_nt(q4, cmp_ref[g][:, :HEAD_DIM])
        c_end = lax.broadcasted_iota(jnp.int32, (1, ncmp), 1) * CMP_STRIDE + (2 * CMP_STRIDE - 1)
        p = _softmax_masked(s, c_end <= past + t4)
        o_cmp = _dot(p.astype(BF16), cmp_ref[KV_HEADS + g][:, :HEAD_DIM])

        imp4 = _split_dot(p, pool_ref[...])
        imp = imp4[0:tnew]
        for n in range(1, GROUP):
            imp = imp + imp4[n * tnew:(n + 1) * tnew]
        if nlane > nblk:
            imp = jnp.concatenate([imp, jnp.zeros((tnew, nlane - nblk), F32)], axis=1)
        score = jnp.where(visible, jnp.where(forced, FORCED_SCORE, imp), -1.0)
        rank = jnp.zeros((tnew, nlane), F32)
        for k in range(nblk + 1):
            ck = score[:, k:k + 1]
            beats = (ck > score) | ((ck == score) & (blk > k))
            rank = rank + jnp.where(beats, 1.0, 0.0)
        sel = jnp.where(visible & (rank < float(SLC_TOP_N)), 1.0, 0.0)

        q4r = _stack_heads(qr, g).astype(BF16)
        ksp_t = buf_ref[slot, HEAD_DIM * g:HEAD_DIM * (g + 1), :].astype(BF16)
        vsp_t = buf_ref[slot, KV_WIDTH + HEAD_DIM * g:KV_WIDTH + HEAD_DIM * (g + 1), :].astype(BF16)
        sel4 = jnp.concatenate([sel[:, :nblk]] * GROUP, axis=0).astype(BF16)
        mpast = _dot(sel4, e_ref[...]) > 0.5
        ks_new = new16(nsa_new, 2 * KV_WIDTH + HEAD_DIM * g)
        vs_new = new16(nsa_new, 3 * KV_WIDTH + HEAD_DIM * g)
        cur_sel4 = jnp.concatenate([sel[:, nblk:nblk + 1]] * GROUP, axis=0) > 0.5
        o_slc = _two_part_attention(_dot(q4r, ksp_t), mpast, vsp_t,
                                    _dot_nt(q4r, ks_new), new_mask & cur_sel4, vs_new)

        nbuf = winc_ref.shape[1]
        kwp_t = winc_ref[HEAD_DIM * g:HEAD_DIM * (g + 1), :].astype(BF16)
        vwp_t = winc_ref[KV_WIDTH + HEAD_DIM * g:KV_WIDTH + HEAD_DIM * (g + 1), :].astype(BF16)
        kw_new = new16(win_new, HEAD_DIM * g)
        vw_new = new16(win_new, KV_WIDTH + HEAD_DIM * g)
        diff = (t4 + nbuf) - lax.broadcasted_iota(jnp.int32, (1, nbuf), 1)
        o_win = _two_part_attention(_dot(q4r, kwp_t), (diff >= 0) & (diff < WINDOW), vwp_t,
                                    _dot_nt(q4r, kw_new), new_mask, vw_new)

        for n in range(GROUP):
            hd = GROUP * g + n
            r = slice(n * tnew, (n + 1) * tnew)
            o = (o_cmp[r] * gates[:, 3 * hd:3 * hd + 1] + o_slc[r] * gates[:, 3 * hd + 1:3 * hd + 2]
                 + o_win[r] * gates[:, 3 * hd + 2:3 * hd + 3])
            o_ref[:, HEAD_DIM * hd:HEAD_DIM * (hd + 1)] = o


def _attn_sample(page_table, cache_t, qc, qr, gates, cmp, nsa_new, win_new, winc_t, pool, expand, layer, tnew):
    nb, npages = page_table.shape
    page = cache_t.shape[3]
    past = npages * page
    ncmp = cmp.shape[2]
    nblk = past // SLC_BLOCK
    nbuf = winc_t.shape[3]
    body = functools.partial(_attn_sample_body, layer=layer, page=page, npages=npages, tnew=tnew)
    per_b = lambda w: pl.BlockSpec((None, tnew, w), lambda b, pt: (b, 0, 0))
    return pl.pallas_call(
        body,
        grid_spec=pltpu.PrefetchScalarGridSpec(
            num_scalar_prefetch=1,
            grid=(nb,),
            in_specs=[
                pl.BlockSpec(memory_space=pl.ANY),
                per_b(NSA_WIDTH), per_b(NSA_WIDTH), per_b(LANES),
                pl.BlockSpec((None, 4, ncmp, LANES), lambda b, pt: (b, 0, 0, 0)),
                per_b(4 * KV_WIDTH), per_b(2 * KV_WIDTH),
                pl.BlockSpec((None, None, 2 * KV_WIDTH, nbuf), lambda b, pt: (layer, b, 0, 0)),
                pl.BlockSpec((ncmp, nblk), lambda b, pt: (0, 0)),
                pl.BlockSpec((nblk, past), lambda b, pt: (0, 0)),
            ],
            out_specs=per_b(NSA_WIDTH),
            scratch_shapes=[
                pltpu.VMEM((2, 2 * KV_WIDTH, past), F32),
                pltpu.SemaphoreType.DMA((2,)),
            ],
        ),
        out_shape=jax.ShapeDtypeStruct((nb, tnew, NSA_WIDTH), F32),
        compiler_params=_cparams(("arbitrary",)),
        name="attn_sample",
    )(page_table, cache_t, qc, qr, gates, cmp, nsa_new, win_new, winc_t, pool, expand)


def _rope_tables(pos):
    half = HEAD_DIM // 2
    inv_freq = ROPE_THETA ** (-jnp.arange(half, dtype=F32) / half)
    ang = pos.astype(F32)[:, None] * inv_freq[None, :]
    cos = jnp.cos(ang)
    sin = jnp.sin(ang)
    cos_t = jnp.tile(jnp.concatenate([cos, cos], axis=-1), (1, LANES // HEAD_DIM))
    sin_t = jnp.tile(jnp.concatenate([-sin, sin], axis=-1), (1, LANES // HEAD_DIM))
    return cos_t, sin_t


def _pack_layer(l, attn_norm_g, w_in, q_norm_g, k_norm_g, cmp_pos, cmp_w1, cmp_b1, cmp_w2,
                hg_norm_g, mlp_norm_g):
    w = w_in[l]
    gate_lo = COL_GATE
    gate_hi = COL_GATE + 3 * NSA_HEADS
    w_gate = jnp.pad(w[:, gate_lo:gate_hi], ((0, 0), (0, LANES - 3 * NSA_HEADS)))
    w_pack = jnp.concatenate([w[:, :gate_lo], w_gate, w[:, gate_hi:]], axis=1).astype(BF16)
    tile2 = lambda v: jnp.tile(v, LANES // HEAD_DIM)[None, :]
    half = CMP_STRIDE * HEAD_DIM
    w1 = cmp_w1[l]
    w1cat = jnp.concatenate([w1[:, :half], w1[:, half:]], axis=-1).astype(BF16)
    pe = cmp_pos[l].reshape(2, 2, 1, half)
    pe16 = jnp.broadcast_to(pe, (2, 2, 8, half)).reshape(2, 16, half)
    w2 = cmp_w2[l]
    w2pad = jnp.concatenate([w2, jnp.zeros_like(w2)], axis=-1).astype(BF16)
    return dict(
        g1=attn_norm_g[l][None, :], w_pack=w_pack, gq=tile2(q_norm_g[l]),
        gk=jnp.concatenate([tile2(k_norm_g[l, 1]), tile2(k_norm_g[l, 2])], axis=0),
        cw=(pe16, w1cat, cmp_b1[l][:, None, :], w2pad, tile2(k_norm_g[l, 0])),
        hg_gain=hg_norm_g[l][None, :], g2=mlp_norm_g[l][None, :],
    )


def _same_head_matrix():
    i = np.arange(LANES)
    return jnp.asarray((i[:, None] // HEAD_DIM) == (i[None, :] // HEAD_DIM), BF16)


def _pool_matrix(ncmp_rows, nblk):
    c = np.arange(ncmp_rows)
    j = np.arange(nblk)
    return jnp.asarray((c[:, None] // (SLC_BLOCK // CMP_STRIDE)) == j[None, :], BF16)


def _expand_matrix(nblk, nkeys):
    j = np.arange(nblk)
    s = np.arange(nkeys)
    return jnp.asarray(j[:, None] == (s[None, :] // SLC_BLOCK), BF16)


def _block_code_table(npos):
    lane = np.arange(LANES)[None, :]
    blk = (np.arange(npos) // SLC_BLOCK)[:, None]
    return jnp.asarray(np.where(lane == HEAD_DIM + blk, BLOCK_CODE, 0.0), F32)


def _feature_major(cache):
    lead = cache.shape[:-4]
    rows = cache.shape[-4]
    n = len(lead)
    perm = tuple(range(n)) + (n + 1, n + 2, n + 3, n)
    return jnp.transpose(cache, perm).reshape(*lead, -1, rows)


def kernel(x_prompt, x_sample, cache_nsa_kv, cache_win_kv, state_hgrn, page_table, attn_norm_g, w_in,
           q_norm_g, k_norm_g, cmp_pos, cmp_w1, cmp_b1, cmp_w2, hg_lb_param, hg_norm_g, w_out,
           mlp_norm_g, w_up, w_down):
    depth = w_in.shape[0]
    batch, seq, _ = x_prompt.shape
    nb, tnew, _ = x_sample.shape
    npool, page = cache_nsa_kv.shape[1], cache_nsa_kv.shape[2]
    npages = page_table.shape[1]
    past = npages * page
    nbuf = cache_win_kv.shape[2]

    lb_all = jnp.cumsum(jax.nn.softmax(hg_lb_param.astype(F32), axis=0), axis=0)
    lb_all = lb_all - lb_all[0]

    cos_p, sin_p = _rope_tables(jnp.arange(seq, dtype=jnp.int32))
    cos_s, sin_s = _rope_tables(past + jnp.arange(tnew, dtype=jnp.int32))
    cos_s = jnp.tile(cos_s, (nb, 1))
    sin_s = jnp.tile(sin_s, (nb, 1))
    bd = _same_head_matrix()
    nsel_p = seq // SLC_BLOCK
    poolt_p = _pool_matrix(seq // CMP_STRIDE, nsel_p).T
    code_p = _block_code_table(seq)
    code_s = jnp.zeros((nb * tnew, LANES), F32)
    pool_s = _pool_matrix(past // CMP_STRIDE, past // SLC_BLOCK)
    expand_s = _expand_matrix(past // SLC_BLOCK, past)

    cache_t = _feature_major(cache_nsa_kv)
    winc_t = _feature_major(cache_win_kv)
    zero_state = jnp.zeros((batch, HG_HEADS, HG_DK, HG_DK), F32)
    wo_all, wu_all, wd_all = w_out.astype(BF16), w_up.astype(BF16), w_down.astype(BF16)

    tq = 256
    hg_tile = 128
    xp = x_prompt.reshape(batch * seq, D_MODEL)
    xs = x_sample.reshape(nb * tnew, D_MODEL)
    p_nsa, p_win, p_hg, s_nsa, s_win, s_hg = [], [], [], [], [], []
    for l in range(depth):
        pk = _pack_layer(l, attn_norm_g, w_in, q_norm_g, k_norm_g, cmp_pos, cmp_w1, cmp_b1, cmp_w2,
                         hg_norm_g, mlp_norm_g)
        lb = lb_all[l][None, :]

        qc, qr, nsa, win, kva, gates, hgin = _proj(xp, pk["g1"], pk["w_pack"], cos_p, sin_p, code_p, pk["gq"], pk["gk"], bd, 512)
        cmp = _compress_prompt(nsa, batch, seq, pk["cw"] + (bd,))
        o_nsa = _attn_prompt(qc, qr, gates, cmp, kva, poolt_p, batch, seq, tq, 512)
        o_hg, hfin = _hgrn(hgin, lb, pk["hg_gain"], zero_state, batch, seq, hg_tile, hg_tile)
        xp = _mlp(xp, o_nsa, o_hg, wo_all, pk["g2"], wu_all, wd_all, l, 512)
        p_nsa.append(nsa.reshape(batch, seq, 4, KV_HEADS, HEAD_DIM))
        p_win.append(win.reshape(batch, seq, 2, KV_HEADS, HEAD_DIM)[:, seq - min(WINDOW, seq):])
        p_hg.append(hfin)

        ns = nb * tnew
        qc, qr, nsa, win, _, gates, hgin = _proj(xs, pk["g1"], pk["w_pack"], cos_s, sin_s, code_s, pk["gq"], pk["gk"], bd, ns)
        cmp = _compress_sample(page_table, cache_t, pk["cw"] + (bd,), l)
        r3 = lambda a: a.reshape(nb, tnew, a.shape[-1])
        o_nsa = _attn_sample(page_table, cache_t, r3(qc), r3(qr), r3(gates), cmp, r3(nsa), r3(win), winc_t,
                             pool_s, expand_s, l, tnew)
        hg_pad = jnp.pad(r3(hgin), ((0, 0), (0, hg_tile - tnew), (0, 0))).reshape(nb * hg_tile, 4 * HG_WIDTH)
        o_hg, hfin = _hgrn(hg_pad, lb, pk["hg_gain"], state_hgrn[l], nb, hg_tile, hg_tile, tnew)
        o_hg = o_hg.reshape(nb, hg_tile, HG_WIDTH)[:, :tnew].reshape(ns, HG_WIDTH)
        xs = _mlp(xs, o_nsa.reshape(ns, NSA_WIDTH).astype(BF16), o_hg, wo_all, pk["g2"], wu_all, wd_all, l, ns)
        s_nsa.append(nsa.reshape(nb, tnew, 4, KV_HEADS, HEAD_DIM))
        s_win.append(win.reshape(nb, tnew, 2, KV_HEADS, HEAD_DIM))
        s_hg.append(hfin)

    return (xp.reshape(batch, seq, D_MODEL), xs.reshape(nb, tnew, D_MODEL),
            jnp.stack(p_nsa), jnp.stack(p_win), jnp.stack(p_hg),
            jnp.stack(s_nsa), jnp.stack(s_win), jnp.stack(s_hg))
```

```python
import functools

import numpy as np
import jax
import jax.numpy as jnp
from jax import lax
from jax.experimental import pallas as pl
from jax.experimental.pallas import tpu as pltpu

F32 = jnp.float32
BF16 = jnp.bfloat16

D_MODEL = 1024
HEAD_DIM = 64
NSA_HEADS = 8
KV_HEADS = 2
GROUP = NSA_HEADS // KV_HEADS
NSA_WIDTH = NSA_HEADS * HEAD_DIM
KV_WIDTH = KV_HEADS * HEAD_DIM
CMP_STRIDE = 16
CMP_HIDDEN = 4 * HEAD_DIM
SLC_BLOCK = 64
SLC_TOP_N = 16
WINDOW = 512
FORCED_SCORE = 1.0e4
HG_HEADS = 4
HG_DK = 128
HG_WIDTH = HG_HEADS * HG_DK
HG_CHUNK = 32
D_FF = 4 * D_MODEL
ROPE_THETA = 10000.0
EPS = 1e-6
SCALE = HEAD_DIM ** -0.5
LOG2E = 1.4426950408889634
NEG = -1e30
BLOCK_CODE = -(2.0 ** 100)

LANES = 128
PA_W = NSA_WIDTH + 6 * KV_WIDTH + LANES
COL_KC, COL_VC, COL_KS, COL_VS, COL_KW, COL_VW = (NSA_WIDTH + i * KV_WIDTH for i in range(6))
COL_GATE = NSA_WIDTH + 6 * KV_WIDTH
VMEM_LIMIT = 56 * 1024 * 1024


def _cparams(sem):
    return pltpu.CompilerParams(dimension_semantics=sem, vmem_limit_bytes=VMEM_LIMIT)


def _dot(a, b):
    return jnp.dot(a, b, preferred_element_type=F32)


def _dot_nt(a, b):
    return lax.dot_general(a, b, (((1,), (1,)), ((), ())), preferred_element_type=F32)


def _split_dot(a, b, nt=False):
    hi = a.astype(BF16)
    lo = (a - hi.astype(F32)).astype(BF16)
    f = _dot_nt if nt else _dot
    return f(hi, b) + f(lo, b)


def _head_norm(v, bd, gain):
    ss = _split_dot(v * v, bd)
    return v * lax.rsqrt(ss * (1.0 / HEAD_DIM) + EPS) * gain


def _rope(v, cos, sin, first_half):
    sw = jnp.where(first_half, pltpu.roll(v, LANES - HEAD_DIM // 2, 1), pltpu.roll(v, HEAD_DIM // 2, 1))
    return v * cos + sw * sin


def _sigmoid(x):
    return 1.0 / (1.0 + jnp.exp(-x))


def _proj_body(x_ref, g_ref, w_ref, cos_ref, sin_ref, code_ref, gq_ref, gk_ref, bd_ref,
               qc_ref, qr_ref, nsa_ref, win_ref, kva_ref, gate_ref, hgin_ref):
    x = x_ref[...]
    ms = jnp.mean(x * x, axis=-1, keepdims=True)
    h = (x * lax.rsqrt(ms + EPS) * g_ref[...]).astype(BF16)
    pa = _dot(h, w_ref[:, :PA_W])
    hgin_ref[...] = _dot(h, w_ref[:, PA_W:])
    cos = cos_ref[...]
    sin = sin_ref[...]
    bd = bd_ref[...]
    lane = lax.broadcasted_iota(jnp.int32, cos.shape, 1)
    first_half = (lane % HEAD_DIM) < (HEAD_DIM // 2)
    gq = gq_ref[...]
    for c in range(NSA_WIDTH // LANES):
        sl = slice(LANES * c, LANES * (c + 1))
        qn = _head_norm(pa[:, sl], bd, gq) * (SCALE * LOG2E)
        qc_ref[:, sl] = qn.astype(BF16)
        qr_ref[:, sl] = _rope(qn, cos, sin, first_half).astype(BF16)
    kc = pa[:, COL_KC:COL_KC + KV_WIDTH]
    vc = pa[:, COL_VC:COL_VC + KV_WIDTH]
    ks = _rope(_head_norm(pa[:, COL_KS:COL_KS + KV_WIDTH], bd, gk_ref[0:1, :]), cos, sin, first_half)
    vs = pa[:, COL_VS:COL_VS + KV_WIDTH]
    kw = _rope(_head_norm(pa[:, COL_KW:COL_KW + KV_WIDTH], bd, gk_ref[1:2, :]), cos, sin, first_half)
    vw = pa[:, COL_VW:COL_VW + KV_WIDTH]
    for i, a in enumerate((kc, vc, ks, vs)):
        nsa_ref[:, LANES * i:LANES * (i + 1)] = a
    win_ref[:, 0:LANES] = kw
    win_ref[:, LANES:2 * LANES] = vw
    low = lane < HEAD_DIM
    code = code_ref[...]
    aug = []
    for kk, vv, fill in ((ks, vs, code), (kw, vw, 0.0)):
        aug += [jnp.where(low, kk, fill), jnp.where(low, pltpu.roll(kk, HEAD_DIM, 1), fill),
                jnp.where(low, vv, 1.0), jnp.where(low, pltpu.roll(vv, HEAD_DIM, 1), 1.0)]
    for i, a in enumerate(aug):
        kva_ref[:, LANES * i:LANES * (i + 1)] = a.astype(BF16)
    gate_ref[...] = _sigmoid(pa[:, COL_GATE:COL_GATE + LANES])


def _proj(x, g, w, cos, sin, code, gq, gk, bd, tn):
    n = x.shape[0]
    nt = cos.shape[0] // tn
    wtot = w.shape[1]
    row = lambda i: (i, 0)
    const = lambda i: (0, 0)
    outs = [
        jax.ShapeDtypeStruct((n, NSA_WIDTH), BF16),
        jax.ShapeDtypeStruct((n, NSA_WIDTH), BF16),
        jax.ShapeDtypeStruct((n, 4 * KV_WIDTH), F32),
        jax.ShapeDtypeStruct((n, 2 * KV_WIDTH), F32),
        jax.ShapeDtypeStruct((n, 8 * LANES), BF16),
        jax.ShapeDtypeStruct((n, LANES), F32),
        jax.ShapeDtypeStruct((n, 4 * HG_WIDTH), F32),
    ]
    return pl.pallas_call(
        _proj_body,
        grid=(n // tn,),
        in_specs=[
            pl.BlockSpec((tn, D_MODEL), row),
            pl.BlockSpec((1, D_MODEL), const),
            pl.BlockSpec((D_MODEL, wtot), const),
            pl.BlockSpec((tn, LANES), lambda i: (i % nt, 0)),
            pl.BlockSpec((tn, LANES), lambda i: (i % nt, 0)),
            pl.BlockSpec((tn, LANES), lambda i: (i % nt, 0)),
            pl.BlockSpec((1, LANES), const),
            pl.BlockSpec((2, LANES), const),
            pl.BlockSpec((LANES, LANES), const),
        ],
        out_specs=[pl.BlockSpec((tn, o.shape[1]), row) for o in outs],
        out_shape=outs,
        compiler_params=_cparams(("parallel",)),
        name="proj",
    )(x, g, w, cos, sin, code, gq, gk, bd)


def _gelu_tanh(x):
    return 0.5 * x * (1.0 + jnp.tanh(0.7978845608028654 * (x + 0.044715 * (x * x * x))))


def _fill_flat(load_rows, flat_ref, nch):
    for j in range(CMP_STRIDE):
        for slot in range(2):
            rows = load_rows(j, slot)
            for g in range(KV_HEADS):
                flat_ref[2 * slot + g, :, HEAD_DIM * j:HEAD_DIM * (j + 1)] = rows[:, HEAD_DIM * g:HEAD_DIM * (g + 1)]


def _compress_core(flat_ref, nch, nvalid, pe_ref, w1_ref, b1_ref, w2_ref, kn_ref, bd_ref, out_ref):
    rowid = lax.broadcasted_iota(jnp.int32, (nch, LANES), 0)
    low = lax.broadcasted_iota(jnp.int32, (nch, LANES), 1) < HEAD_DIM
    for slot in range(2):
        w1 = w1_ref[slot]
        pe = pe_ref[slot]
        pw = _split_dot(pe, w1)
        const = pw[0:1, :CMP_HIDDEN] + pw[8:9, CMP_HIDDEN:] + b1_ref[slot]
        for g in range(KV_HEADS):
            f = flat_ref[2 * slot + g].astype(BF16)
            ab = _dot(f, w1)
            pre = ab[:, :CMP_HIDDEN] + pltpu.roll(ab[:, CMP_HIDDEN:], nch - 1, 0) + const
            hid = _gelu_tanh(pre).astype(BF16)
            o = _dot(hid, w2_ref[slot])
            if slot == 0:
                o = _head_norm(o, bd_ref[...], kn_ref[...])
            else:
                o = jnp.where(low, o, 1.0)
            out_ref[2 * slot + g] = jnp.where(rowid < nvalid, o, 0.0).astype(BF16)


def _compress_prompt_body(xk_ref, xv_ref, pe_ref, w1_ref, b1_ref, w2_ref, kn_ref, bd_ref, out_ref, flat_ref):
    nch = flat_ref.shape[1]
    _fill_flat(lambda j, slot: (xk_ref, xv_ref)[slot][pl.ds(j, nch, stride=CMP_STRIDE), :], flat_ref, nch)
    _compress_core(flat_ref, nch, nch - 1, pe_ref, w1_ref, b1_ref, w2_ref, kn_ref, bd_ref, out_ref)


def _cmp_weight_specs(nidx):
    z = lambda *a: (0,) * nidx
    return [
        pl.BlockSpec((2, 16, CMP_STRIDE * HEAD_DIM), lambda *a: (0, 0, 0)),
        pl.BlockSpec((2, CMP_STRIDE * HEAD_DIM, 2 * CMP_HIDDEN), lambda *a: (0, 0, 0)),
        pl.BlockSpec((2, 1, CMP_HIDDEN), lambda *a: (0, 0, 0)),
        pl.BlockSpec((2, CMP_HIDDEN, LANES), lambda *a: (0, 0, 0)),
        pl.BlockSpec((1, LANES), lambda *a: (0, 0)),
        pl.BlockSpec((LANES, LANES), lambda *a: (0, 0)),
    ]


def _compress_prompt(nsa, batch, seq, cw):
    nch = seq // CMP_STRIDE
    return pl.pallas_call(
        _compress_prompt_body,
        grid=(batch,),
        in_specs=[pl.BlockSpec((seq, KV_WIDTH), lambda b: (b, 0)),
                  pl.BlockSpec((seq, KV_WIDTH), lambda b: (b, 1))] + _cmp_weight_specs(1),
        out_specs=pl.BlockSpec((None, 4, nch, LANES), lambda b: (b, 0, 0, 0)),
        out_shape=jax.ShapeDtypeStruct((batch, 4, nch, LANES), BF16),
        scratch_shapes=[pltpu.VMEM((4, nch, CMP_STRIDE * HEAD_DIM), F32)],
        compiler_params=_cparams(("parallel",)),
        name="compress_prompt",
    )(nsa, nsa, *cw)


def _softmax_masked(s, mask):
    sm = jnp.where(mask, s, NEG)
    m = jnp.max(sm, axis=-1, keepdims=True)
    e = jnp.where(mask, jnp.exp2(sm - m), 0.0)
    return e / jnp.maximum(jnp.sum(e, axis=-1, keepdims=True), 1e-30)


def _stack_heads(q, g, rows_f32=False):
    parts = [q[:, HEAD_DIM * (GROUP * g + n):HEAD_DIM * (GROUP * g + n + 1)] for n in range(GROUP)]
    return jnp.concatenate(parts, axis=0)


def _place_heads(q, g, fill, low):
    parts = []
    for n in range(GROUP):
        hd = GROUP * g + n
        chunk = q[:, LANES * (hd // 2):LANES * (hd // 2 + 1)].astype(F32)
        if hd % 2:
            chunk = pltpu.roll(chunk, HEAD_DIM, 1)
        parts.append(jnp.where(low, chunk, fill).astype(BF16))
    return jnp.concatenate(parts, axis=0)


def _attend(q4, k, v, bias):
    s = _add_bias(_dot_nt(q4, k), bias)
    e = jnp.exp2(s - jnp.max(s, axis=-1, keepdims=True))
    return _dot(e.astype(BF16), v), e


def _add_bias(s, bias):
    if bias is None:
        return s
    tq, width = bias.shape
    return (s.reshape(GROUP, tq, width) + bias[None]).reshape(GROUP * tq, width)


def _attn_prompt_body(qc_ref, qr_ref, gate_ref, cmp_ref, slc_ref, win_ref, poolt_ref, o_ref, score_ref,
                      *, tq, seq, kchunk):
    i = pl.program_id(1)
    q0 = i * tq
    ncmp = cmp_ref.shape[1]
    nsel = seq // SLC_BLOCK
    m4 = GROUP * tq
    qc = qc_ref[...]
    qr = qr_ref[...]
    gates = gate_ref[...]
    low = lax.broadcasted_iota(jnp.int32, (tq, LANES), 1) < HEAD_DIM
    qpos = q0 + lax.broadcasted_iota(jnp.int32, (tq, 1), 0)
    rep4 = lambda a: jnp.concatenate([a] * GROUP, axis=0)

    c_end = lax.broadcasted_iota(jnp.int32, (1, ncmp), 1) * CMP_STRIDE + (2 * CMP_STRIDE - 1)
    cmp_bias = jnp.where(c_end <= qpos, 0.0, NEG)
    cmp_any = rep4(jnp.where(qpos >= 2 * CMP_STRIDE - 1, 1.0, 0.0))
    wlen = WINDOW + tq
    w0 = pl.multiple_of(jnp.maximum(q0 - WINDOW, 0), tq)
    diff = qpos - (w0 + lax.broadcasted_iota(jnp.int32, (1, wlen), 1))
    win_bias = jnp.where((diff >= 0) & (diff < WINDOW), 0.0, NEG)
    nk = (q0 + tq + kchunk - 1) // kchunk
    kpos_d = (nk - 1) * kchunk + lax.broadcasted_iota(jnp.int32, (1, kchunk), 1)
    diag_bias = jnp.where(kpos_d <= qpos, 0.0, NEG)

    blk = lax.broadcasted_iota(jnp.int32, (nsel, tq), 0)
    qp_l = q0 + lax.broadcasted_iota(jnp.int32, (nsel, tq), 1)
    cur = qp_l // SLC_BLOCK
    forced = (blk == 0) | (blk == cur) | (blk == cur - 1)
    visible = blk * SLC_BLOCK <= qp_l
    nvis = jnp.minimum((q0 + tq + SLC_BLOCK - 1) // SLC_BLOCK, nsel)

    kcol = lambda g: slice(LANES * g, LANES * (g + 1))
    vcol = lambda g: slice(LANES * (KV_HEADS + g), LANES * (KV_HEADS + g + 1))

    for g in range(KV_HEADS):
        acc_c, e_c = _attend(_place_heads(qc, g, 0.0, low), cmp_ref[g], cmp_ref[KV_HEADS + g], cmp_bias)
        inv_c = cmp_any / acc_c[:, HEAD_DIM:HEAD_DIM + 1]
        o_cmp = acc_c * inv_c
        p = e_c * inv_c
        psum = p[0:tq] + p[tq:2 * tq] + p[2 * tq:3 * tq] + p[3 * tq:4 * tq]

        p_hi = psum.astype(BF16)
        p_lo = (psum - p_hi.astype(F32)).astype(BF16)
        imp_t = _dot_nt(poolt_ref[...], p_hi) + _dot_nt(poolt_ref[...], p_lo)
        score = jnp.where(visible, jnp.where(forced, FORCED_SCORE, imp_t), -1.0)
        score_ref[...] = score

        def rank_step(k, rank):
            rk = score_ref[pl.ds(k, 1), :]
            return rank + jnp.where((rk > score) | ((rk == score) & (blk > k)), 1.0, 0.0)

        rank = lax.fori_loop(0, nvis, rank_step, jnp.zeros((nsel, tq), F32))
        notsel = jnp.where(visible & (rank < float(SLC_TOP_N)), 0.0, 1.0)
        pieces = [jnp.zeros((HEAD_DIM, tq), F32), notsel]
        if nsel < HEAD_DIM:
            pieces.append(jnp.ones((HEAD_DIM - nsel, tq), F32))
        fill = jnp.concatenate(pieces, axis=0).T

        q4r = _place_heads(qr, g, fill, low)

        def absorb(c, m_i, acc, bias):
            k0 = pl.multiple_of(c * kchunk, kchunk)
            s = _add_bias(_dot_nt(q4r, slc_ref[pl.ds(k0, kchunk), kcol(g)]), bias)
            m_new = jnp.maximum(m_i, jnp.max(s, axis=-1, keepdims=True))
            e = jnp.exp2(s - m_new).astype(BF16)
            return m_new, jnp.exp2(m_i - m_new) * acc + _dot(e, slc_ref[pl.ds(k0, kchunk), vcol(g)])

        st = (jnp.full((m4, 1), NEG, F32), jnp.zeros((m4, LANES), F32))
        m_l, acc_l = lax.fori_loop(0, nk - 1, lambda c, cr: absorb(c, cr[0], cr[1], None), st)
        _, acc_s = absorb(nk - 1, m_l, acc_l, diag_bias)
        o_slc = acc_s / acc_s[:, HEAD_DIM:HEAD_DIM + 1]

        acc_w, _ = _attend(q4r, win_ref[pl.ds(w0, wlen), kcol(g)], win_ref[pl.ds(w0, wlen), vcol(g)], win_bias)
        o_win = acc_w / acc_w[:, HEAD_DIM:HEAD_DIM + 1]

        comb = []
        for n in range(GROUP):
            hd = GROUP * g + n
            r = slice(n * tq, (n + 1) * tq)
            comb.append(o_cmp[r] * gates[:, 3 * hd:3 * hd + 1] + o_slc[r] * gates[:, 3 * hd + 1:3 * hd + 2]
                        + o_win[r] * gates[:, 3 * hd + 2:3 * hd + 3])
        for pair in range(GROUP // 2):
            both = jnp.where(low, comb[2 * pair], pltpu.roll(comb[2 * pair + 1], HEAD_DIM, 1))
            c0 = LANES * ((GROUP // 2) * g + pair)
            o_ref[:, c0:c0 + LANES] = both.astype(BF16)


def _attn_prompt(qc, qr, gates, cmp, kva, poolt, batch, seq, tq, kchunk):
    n = batch * seq
    nq = seq // tq
    ncmp = cmp.shape[2]
    nsel = seq // SLC_BLOCK
    assert nsel <= HEAD_DIM, "block codes use the 64 spare contraction lanes"
    body = functools.partial(_attn_prompt_body, tq=tq, seq=seq, kchunk=kchunk)
    return pl.pallas_call(
        body,
        grid=(batch, nq),
        in_specs=[
            pl.BlockSpec((tq, NSA_WIDTH), lambda b, i: (b * nq + i, 0)),
            pl.BlockSpec((tq, NSA_WIDTH), lambda b, i: (b * nq + i, 0)),
            pl.BlockSpec((tq, LANES), lambda b, i: (b * nq + i, 0)),
            pl.BlockSpec((None, 4, ncmp, LANES), lambda b, i: (b, 0, 0, 0)),
            pl.BlockSpec((seq, 4 * LANES), lambda b, i: (b, 0)),
            pl.BlockSpec((seq, 4 * LANES), lambda b, i: (b, 1)),
            pl.BlockSpec((nsel, ncmp), lambda b, i: (0, 0)),
        ],
        out_specs=pl.BlockSpec((tq, NSA_WIDTH), lambda b, i: (b * nq + i, 0)),
        out_shape=jax.ShapeDtypeStruct((n, NSA_WIDTH), BF16),
        scratch_shapes=[pltpu.VMEM((nsel, tq), F32)],
        compiler_params=_cparams(("parallel", "parallel")),
        name="attn_prompt",
    )(qc, qr, gates, cmp, kva, kva, poolt)


def _hgrn_body(x_ref, lb_ref, gain_ref, s0_ref, o_ref, sfin_ref, st_ref, *, tile, nvalid, bstep):
    i = pl.program_id(1)
    nsub = tile // HG_CHUNK

    @pl.when(i == 0)
    def _():
        for bb in range(bstep):
            for hd in range(HG_HEADS):
                st_ref[bb * HG_HEADS + hd] = s0_ref[bb, hd].T

    for bb in range(bstep):
        _hgrn_tile(x_ref.at[bb], lb_ref, gain_ref, o_ref.at[bb], st_ref, bb * HG_HEADS, tile, nvalid, nsub)

    @pl.when(i == pl.num_programs(1) - 1)
    def _():
        for bb in range(bstep):
            for hd in range(HG_HEADS):
                sfin_ref[bb, hd] = st_ref[bb * HG_HEADS + hd].T


def _hgrn_tile(x_ref, lb_ref, gain_ref, o_ref, st_ref, st0, tile, nvalid, nsub):
    if nvalid < tile:
        x_all = jnp.concatenate([x_ref[...], jnp.zeros((tile - nvalid, 4 * HG_WIDTH), F32)], axis=0)
    row = lax.broadcasted_iota(jnp.int32, (tile, HG_DK), 0)
    rowc = row % HG_CHUNK
    col = lax.broadcasted_iota(jnp.int32, (tile, tile), 1)
    rr = lax.broadcasted_iota(jnp.int32, (tile, tile), 0)
    intra_mask = (rr // HG_CHUNK == col // HG_CHUNK) & (rr >= col)
    lane_t = lax.broadcasted_iota(jnp.int32, (HG_DK, tile), 1)
    valid = row < nvalid

    for hd in range(HG_HEADS):
        sl = lambda k: slice(k * HG_WIDTH + hd * HG_DK, k * HG_WIDTH + (hd + 1) * HG_DK)
        src = x_ref if nvalid == tile else x_all
        q = src[:, sl(0)]
        z = src[:, sl(1)]
        v = src[:, sl(2)]
        gt = src[:, sl(3)]
        lb = lb_ref[:, hd * HG_DK:(hd + 1) * HG_DK]
        log_sig = jnp.minimum(z, 0.0) - jnp.log(1.0 + jnp.exp(-jnp.abs(z)))
        t1 = jnp.log(1.0 - lb) + log_sig
        a = jnp.broadcast_to(jnp.log(lb), t1.shape)
        log_f = jnp.maximum(a, t1) + jnp.log(1.0 + jnp.exp(-jnp.abs(a - t1)))
        key = (1.0 - lb) * _sigmoid(-z)
        if nvalid < tile:
            log_f = jnp.where(valid, log_f, 0.0)
            key = jnp.where(valid, key, 0.0)
            v = jnp.where(valid, v, 0.0)
        qs = q * _sigmoid(q)
        b = log_f
        step = 1
        while step < HG_CHUNK:
            b = b + jnp.where(rowc >= step, pltpu.roll(b, step, 0), 0.0)
            step *= 2
        qe = (qs * jnp.exp(b)).astype(BF16)
        ke = (key * jnp.exp(-b)).astype(BF16)
        vb = v.astype(BF16)
        a_mat = jnp.where(intra_mask, _dot_nt(qe, ke), 0.0)
        o_acc = _dot(a_mat.astype(BF16), vb)
        vt = v.T
        b_lasts = [b[(c + 1) * HG_CHUNK - 1:(c + 1) * HG_CHUNK, :] for c in range(nsub)]
        b_end = jnp.broadcast_to(b_lasts[0], b.shape)
        for c in range(1, nsub):
            b_end = jnp.where(row >= c * HG_CHUNK, b_lasts[c], b_end)
        kd = (key * jnp.exp(b_end - b)).astype(BF16)
        st = st_ref[st0 + hd]
        o_parts = []
        for c in range(nsub):
            r = slice(c * HG_CHUNK, (c + 1) * HG_CHUNK)
            o_parts.append(_dot_nt(qe[r], st.astype(BF16)))
            vtc = jnp.where(lane_t // HG_CHUNK == c, vt, 0.0).astype(BF16)
            st = st * jnp.exp(b_lasts[c]) + _dot(vtc, kd)
        st_ref[st0 + hd] = st
        o = o_acc + jnp.concatenate(o_parts, axis=0)
        ms = jnp.mean(o * o, axis=-1, keepdims=True)
        o = o * lax.rsqrt(ms + EPS) * gain_ref[...] * _sigmoid(gt)
        o_ref[:, hd * HG_DK:(hd + 1) * HG_DK] = o[:nvalid].astype(o_ref.dtype)


def _hgrn(hgin, lb, gain, s0, tile, bstep, out_dtype):
    batch, rows, _ = hgin.shape
    nvalid = min(rows, tile)
    nt = rows // nvalid
    body = functools.partial(_hgrn_body, tile=tile, nvalid=nvalid, bstep=bstep)
    s_spec = pl.BlockSpec((bstep, HG_HEADS, HG_DK, HG_DK), lambda b, i: (b, 0, 0, 0))
    return pl.pallas_call(
        body,
        grid=(batch // bstep, nt),
        in_specs=[
            pl.BlockSpec((bstep, nvalid, 4 * HG_WIDTH), lambda b, i: (b, i, 0)),
            pl.BlockSpec((1, HG_WIDTH), lambda b, i: (0, 0)),
            pl.BlockSpec((1, HG_DK), lambda b, i: (0, 0)),
            s_spec,
        ],
        out_specs=[pl.BlockSpec((bstep, nvalid, HG_WIDTH), lambda b, i: (b, i, 0)), s_spec],
        out_shape=[jax.ShapeDtypeStruct((batch, rows, HG_WIDTH), out_dtype),
                   jax.ShapeDtypeStruct((batch, HG_HEADS, HG_DK, HG_DK), F32)],
        scratch_shapes=[pltpu.VMEM((bstep * HG_HEADS, HG_DK, HG_DK), F32)],
        compiler_params=_cparams(("parallel", "arbitrary")),
        name="hgrn",
    )(hgin, lb, gain, s0)


def _mlp_body(x_ref, on_ref, oh_ref, wo_ref, g_ref, wu_ref, wd_ref, y_ref, *, fchunk):
    x2 = x_ref[...] + _dot(jnp.concatenate([on_ref[...], oh_ref[...]], axis=1), wo_ref[...])
    ms = jnp.mean(x2 * x2, axis=-1, keepdims=True)
    h2 = (x2 * lax.rsqrt(ms + EPS) * g_ref[...]).astype(BF16)
    acc = x2
    for c in range(D_FF // fchunk):
        u = _dot(h2, wu_ref[:, c * fchunk:(c + 1) * fchunk])
        u = jnp.maximum(u, 0.0)
        acc = acc + _dot((u * u).astype(BF16), wd_ref[c * fchunk:(c + 1) * fchunk, :])
    y_ref[...] = acc


def _mlp(x, on, oh, wo, g, wu, wd, layer, tn, fchunk=1024):
    n = x.shape[0]
    row = lambda i: (i, 0)
    const = lambda i: (0, 0)
    slab = lambda i: (layer, 0, 0)
    return pl.pallas_call(
        functools.partial(_mlp_body, fchunk=fchunk),
        grid=(n // tn,),
        in_specs=[
            pl.BlockSpec((tn, D_MODEL), row),
            pl.BlockSpec((tn, NSA_WIDTH), row),
            pl.BlockSpec((tn, HG_WIDTH), row),
            pl.BlockSpec((None, D_MODEL, D_MODEL), slab),
            pl.BlockSpec((1, D_MODEL), const),
            pl.BlockSpec((None, D_MODEL, D_FF), slab),
            pl.BlockSpec((None, D_FF, D_MODEL), slab),
        ],
        out_specs=pl.BlockSpec((tn, D_MODEL), row),
        out_shape=jax.ShapeDtypeStruct((n, D_MODEL), F32),
        compiler_params=_cparams(("parallel",)),
        name="mlp",
    )(x, on, oh, wo, g, wu, wd)


def _pages_pipeline(page_copy, npages):
    b = pl.program_id(0)
    nb = pl.num_programs(0)
    slot = b % 2

    def start(bb, sl):
        def go(p, c):
            page_copy(bb, p, sl).start()
            return c
        lax.fori_loop(0, npages, go, 0)

    @pl.when(b == 0)
    def _():
        start(0, 0)

    @pl.when(b + 1 < nb)
    def _():
        start(b + 1, 1 - slot)

    def wait(p, c):
        page_copy(b, p, slot).wait()
        return c
    lax.fori_loop(0, npages, wait, 0)
    return slot


def _compress_sample_body(pt_ref, cache_ref, pe_ref, w1_ref, b1_ref, w2_ref, kn_ref, bd_ref,
                          out_ref, buf_ref, rows_ref, flat_ref, sem_ref, *, layer, page, npages):
    nfeat = 2 * KV_WIDTH

    def page_copy(b, p, slot):
        return pltpu.make_async_copy(cache_ref.at[layer, pt_ref[b, p], pl.ds(0, nfeat), :],
                                     buf_ref.at[slot, p], sem_ref.at[slot])

    slot = _pages_pipeline(page_copy, npages)
    eye = (lax.broadcasted_iota(jnp.int32, (page, page), 0)
           == lax.broadcasted_iota(jnp.int32, (page, page), 1)).astype(BF16)

    def to_rows(p, c):
        x = _dot_nt(eye, buf_ref[slot, p].astype(BF16))
        r0 = pl.multiple_of(p * page, page)
        rows_ref[0, pl.ds(r0, page), :] = x[:, :KV_WIDTH]
        rows_ref[1, pl.ds(r0, page), :] = x[:, KV_WIDTH:]
        return c
    lax.fori_loop(0, npages, to_rows, 0, unroll=8)
    nch = flat_ref.shape[1]
    _fill_flat(lambda j, s: rows_ref[s, pl.ds(j, nch, stride=CMP_STRIDE), :], flat_ref, nch)
    _compress_core(flat_ref, nch, nch - 1, pe_ref, w1_ref, b1_ref, w2_ref, kn_ref, bd_ref, out_ref)


def _compress_sample(page_table, cache_t, cw, layer):
    nb, npages = page_table.shape
    page = cache_t.shape[3]
    past = npages * page
    nch = past // CMP_STRIDE
    body = functools.partial(_compress_sample_body, layer=layer, page=page, npages=npages)
    return pl.pallas_call(
        body,
        grid_spec=pltpu.PrefetchScalarGridSpec(
            num_scalar_prefetch=1,
            grid=(nb,),
            in_specs=[pl.BlockSpec(memory_space=pl.ANY)] + _cmp_weight_specs(2),
            out_specs=pl.BlockSpec((None, 4, nch, LANES), lambda b, pt: (b, 0, 0, 0)),
            scratch_shapes=[
                pltpu.VMEM((2, npages, 2 * KV_WIDTH, page), F32),
                pltpu.VMEM((2, past, KV_WIDTH), F32),
                pltpu.VMEM((4, nch, CMP_STRIDE * HEAD_DIM), F32),
                pltpu.SemaphoreType.DMA((2,)),
            ],
        ),
        out_shape=jax.ShapeDtypeStruct((nb, 4, nch, LANES), BF16),
        compiler_params=_cparams(("arbitrary",)),
        name="compress_sample",
    )(page_table, cache_t, *cw)


def _two_part_attention(s1, mask1, v1t, s2, mask2, v2):
    s1 = jnp.where(mask1, s1, NEG)
    s2 = jnp.where(mask2, s2, NEG)
    m = jnp.maximum(jnp.max(s1, axis=-1, keepdims=True), jnp.max(s2, axis=-1, keepdims=True))
    e1 = jnp.where(mask1, jnp.exp2(s1 - m), 0.0)
    e2 = jnp.where(mask2, jnp.exp2(s2 - m), 0.0)
    l = jnp.sum(e1, axis=-1, keepdims=True) + jnp.sum(e2, axis=-1, keepdims=True)
    o = _dot_nt(e1.astype(BF16), v1t) + _dot(e2.astype(BF16), v2)
    return o / jnp.maximum(l, 1e-30)


def _attn_sample_body(pt_ref, cache_ref, qc_ref, qr_ref, gate_ref, cmp_ref, nsa_new_ref, win_new_ref, winc_ref,
                      pool_ref, e_ref, o_ref, buf_ref, sem_ref, *, layer, page, npages, tnew):
    nfeat = 2 * KV_WIDTH

    def page_copy(b, p, slot):
        return pltpu.make_async_copy(cache_ref.at[layer, pt_ref[b, p], pl.ds(nfeat, nfeat), :],
                                     buf_ref.at[slot, :, pl.ds(p * page, page)], sem_ref.at[slot])

    slot = _pages_pipeline(page_copy, npages)
    past = npages * page
    ncmp = cmp_ref.shape[1]
    nblk = past // SLC_BLOCK
    nlane = LANES * ((nblk + 1 + LANES - 1) // LANES)
    m4 = GROUP * tnew
    qc = qc_ref[...].astype(F32)
    qr = qr_ref[...].astype(F32)
    gates = gate_ref[...]
    nsa_new = nsa_new_ref[...]
    win_new = win_new_ref[...]
    pad = jnp.zeros((16 - tnew, HEAD_DIM), F32)
    new16 = lambda a, c0: jnp.concatenate([a[:, c0:c0 + HEAD_DIM], pad], axis=0).astype(BF16)
    t4 = lax.broadcasted_iota(jnp.int32, (m4, 1), 0) % tnew
    u16 = lax.broadcasted_iota(jnp.int32, (1, 16), 1)
    new_mask = (u16 <= t4) & (u16 < tnew)

    blk = lax.broadcasted_iota(jnp.int32, (tnew, nlane), 1)
    qpos = past + lax.broadcasted_iota(jnp.int32, (tnew, nlane), 0)
    cur = qpos // SLC_BLOCK
    forced = (blk == 0) | (blk == cur) | (blk == cur - 1)
    visible = (blk * SLC_BLOCK <= qpos) & (blk <= nblk)

    for g in range(KV_HEADS):
        hsl = slice(HEAD_DIM * g, HEAD_DIM * (g + 1))
        q4 = _stack_heads(qc, g).astype(BF16)
        s = _dot_nt(q4, cmp_ref[g][:, :HEAD_DIM])
        c_end = lax.broadcasted_iota(jnp.int32, (1, ncmp), 1) * CMP_STRIDE + (2 * CMP_STRIDE - 1)
        p = _softmax_masked(s, c_end <= past + t4)
        o_cmp = _dot(p.astype(BF16), cmp_ref[KV_HEADS + g][:, :HEAD_DIM])

        imp4 = _split_dot(p, pool_ref[...])
        imp = imp4[0:tnew]
        for n in range(1, GROUP):
            imp = imp + imp4[n * tnew:(n + 1) * tnew]
        if nlane > nblk:
            imp = jnp.concatenate([imp, jnp.zeros((tnew, nlane - nblk), F32)], axis=1)
        score = jnp.where(visible, jnp.where(forced, FORCED_SCORE, imp), -1.0)
        rank = jnp.zeros((tnew, nlane), F32)
        for k in range(nblk + 1):
            ck = score[:, k:k + 1]
            beats = (ck > score) | ((ck == score) & (blk > k))
            rank = rank + jnp.where(beats, 1.0, 0.0)
        sel = jnp.where(visible & (rank < float(SLC_TOP_N)), 1.0, 0.0)

        q4r = _stack_heads(qr, g).astype(BF16)
        ksp_t = buf_ref[slot, HEAD_DIM * g:HEAD_DIM * (g + 1), :].astype(BF16)
        vsp_t = buf_ref[slot, KV_WIDTH + HEAD_DIM * g:KV_WIDTH + HEAD_DIM * (g + 1), :].astype(BF16)
        sel4 = jnp.concatenate([sel[:, :nblk]] * GROUP, axis=0).astype(BF16)
        mpast = _dot(sel4, e_ref[...]) > 0.5
        ks_new = new16(nsa_new, 2 * KV_WIDTH + HEAD_DIM * g)
        vs_new = new16(nsa_new, 3 * KV_WIDTH + HEAD_DIM * g)
        cur_sel4 = jnp.concatenate([sel[:, nblk:nblk + 1]] * GROUP, axis=0) > 0.5
        o_slc = _two_part_attention(_dot(q4r, ksp_t), mpast, vsp_t,
                                    _dot_nt(q4r, ks_new), new_mask & cur_sel4, vs_new)

        nbuf = winc_ref.shape[1]
        kwp_t = winc_ref[HEAD_DIM * g:HEAD_DIM * (g + 1), :].astype(BF16)
        vwp_t = winc_ref[KV_WIDTH + HEAD_DIM * g:KV_WIDTH + HEAD_DIM * (g + 1), :].astype(BF16)
        kw_new = new16(win_new, HEAD_DIM * g)
        vw_new = new16(win_new, KV_WIDTH + HEAD_DIM * g)
        diff = (t4 + nbuf) - lax.broadcasted_iota(jnp.int32, (1, nbuf), 1)
        o_win = _two_part_attention(_dot(q4r, kwp_t), (diff >= 0) & (diff < WINDOW), vwp_t,
                                    _dot_nt(q4r, kw_new), new_mask, vw_new)

        for n in range(GROUP):
            hd = GROUP * g + n
            r = slice(n * tnew, (n + 1) * tnew)
            o = (o_cmp[r] * gates[:, 3 * hd:3 * hd + 1] + o_slc[r] * gates[:, 3 * hd + 1:3 * hd + 2]
                 + o_win[r] * gates[:, 3 * hd + 2:3 * hd + 3])
            o_ref[:, HEAD_DIM * hd:HEAD_DIM * (hd + 1)] = o


def _attn_sample(page_table, cache_t, qc, qr, gates, cmp, nsa_new, win_new, winc_t, pool, expand, layer, tnew):
    nb, npages = page_table.shape
    page = cache_t.shape[3]
    past = npages * page
    ncmp = cmp.shape[2]
    nblk = past // SLC_BLOCK
    nbuf = winc_t.shape[3]
    body = functools.partial(_attn_sample_body, layer=layer, page=page, npages=npages, tnew=tnew)
    per_b = lambda w: pl.BlockSpec((None, tnew, w), lambda b, pt: (b, 0, 0))
    return pl.pallas_call(
        body,
        grid_spec=pltpu.PrefetchScalarGridSpec(
            num_scalar_prefetch=1,
            grid=(nb,),
            in_specs=[
                pl.BlockSpec(memory_space=pl.ANY),
                per_b(NSA_WIDTH), per_b(NSA_WIDTH), per_b(LANES),
                pl.BlockSpec((None, 4, ncmp, LANES), lambda b, pt: (b, 0, 0, 0)),
                per_b(4 * KV_WIDTH), per_b(2 * KV_WIDTH),
                pl.BlockSpec((None, None, 2 * KV_WIDTH, nbuf), lambda b, pt: (layer, b, 0, 0)),
                pl.BlockSpec((ncmp, nblk), lambda b, pt: (0, 0)),
                pl.BlockSpec((nblk, past), lambda b, pt: (0, 0)),
            ],
            out_specs=per_b(NSA_WIDTH),
            scratch_shapes=[
                pltpu.VMEM((2, 2 * KV_WIDTH, past), F32),
                pltpu.SemaphoreType.DMA((2,)),
            ],
        ),
        out_shape=jax.ShapeDtypeStruct((nb, tnew, NSA_WIDTH), F32),
        compiler_params=_cparams(("arbitrary",)),
        name="attn_sample",
    )(page_table, cache_t, qc, qr, gates, cmp, nsa_new, win_new, winc_t, pool, expand)


def _rope_tables(pos):
    half = HEAD_DIM // 2
    inv_freq = ROPE_THETA ** (-jnp.arange(half, dtype=F32) / half)
    ang = pos.astype(F32)[:, None] * inv_freq[None, :]
    cos = jnp.cos(ang)
    sin = jnp.sin(ang)
    cos_t = jnp.tile(jnp.concatenate([cos, cos], axis=-1), (1, LANES // HEAD_DIM))
    sin_t = jnp.tile(jnp.concatenate([-sin, sin], axis=-1), (1, LANES // HEAD_DIM))
    return cos_t, sin_t


def _pack_layer(l, attn_norm_g, w_in, q_norm_g, k_norm_g, cmp_pos, cmp_w1, cmp_b1, cmp_w2,
                hg_norm_g, mlp_norm_g):
    w = w_in[l]
    gate_lo = COL_GATE
    gate_hi = COL_GATE + 3 * NSA_HEADS
    w_gate = jnp.pad(w[:, gate_lo:gate_hi], ((0, 0), (0, LANES - 3 * NSA_HEADS)))
    w_pack = jnp.concatenate([w[:, :gate_lo], w_gate, w[:, gate_hi:]], axis=1).astype(BF16)
    tile2 = lambda v: jnp.tile(v, LANES // HEAD_DIM)[None, :]
    half = CMP_STRIDE * HEAD_DIM
    w1 = cmp_w1[l]
    w1cat = jnp.concatenate([w1[:, :half], w1[:, half:]], axis=-1).astype(BF16)
    pe = cmp_pos[l].reshape(2, 2, 1, half)
    pe16 = jnp.broadcast_to(pe, (2, 2, 8, half)).reshape(2, 16, half)
    w2 = cmp_w2[l]
    w2pad = jnp.concatenate([w2, jnp.zeros_like(w2)], axis=-1).astype(BF16)
    return dict(
        g1=attn_norm_g[l][None, :], w_pack=w_pack, gq=tile2(q_norm_g[l]),
        gk=jnp.concatenate([tile2(k_norm_g[l, 1]), tile2(k_norm_g[l, 2])], axis=0),
        cw=(pe16, w1cat, cmp_b1[l][:, None, :], w2pad, tile2(k_norm_g[l, 0])),
        hg_gain=hg_norm_g[l][None, :], g2=mlp_norm_g[l][None, :],
    )


def _same_head_matrix():
    i = np.arange(LANES)
    return jnp.asarray((i[:, None] // HEAD_DIM) == (i[None, :] // HEAD_DIM), BF16)


def _pool_matrix(ncmp_rows, nblk):
    c = np.arange(ncmp_rows)
    j = np.arange(nblk)
    return jnp.asarray((c[:, None] // (SLC_BLOCK // CMP_STRIDE)) == j[None, :], BF16)


def _expand_matrix(nblk, nkeys):
    j = np.arange(nblk)
    s = np.arange(nkeys)
    return jnp.asarray(j[:, None] == (s[None, :] // SLC_BLOCK), BF16)


def _block_code_table(npos):
    lane = np.arange(LANES)[None, :]
    blk = (np.arange(npos) // SLC_BLOCK)[:, None]
    return jnp.asarray(np.where(lane == HEAD_DIM + blk, BLOCK_CODE, 0.0), F32)


def _feature_major(cache):
    lead = cache.shape[:-4]
    rows = cache.shape[-4]
    n = len(lead)
    perm = tuple(range(n)) + (n + 1, n + 2, n + 3, n)
    return jnp.transpose(cache, perm).reshape(*lead, -1, rows)


def kernel(x_prompt, x_sample, cache_nsa_kv, cache_win_kv, state_hgrn, page_table, attn_norm_g, w_in,
           q_norm_g, k_norm_g, cmp_pos, cmp_w1, cmp_b1, cmp_w2, hg_lb_param, hg_norm_g, w_out,
           mlp_norm_g, w_up, w_down):
    depth = w_in.shape[0]
    batch, seq, _ = x_prompt.shape
    nb, tnew, _ = x_sample.shape
    npool, page = cache_nsa_kv.shape[1], cache_nsa_kv.shape[2]
    npages = page_table.shape[1]
    past = npages * page
    nbuf = cache_win_kv.shape[2]

    lb_all = jnp.cumsum(jax.nn.softmax(hg_lb_param.astype(F32), axis=0), axis=0)
    lb_all = lb_all - lb_all[0]

    cos_p, sin_p = _rope_tables(jnp.arange(seq, dtype=jnp.int32))
    cos_s, sin_s = _rope_tables(past + jnp.arange(tnew, dtype=jnp.int32))
    cos_s = jnp.tile(cos_s, (nb, 1))
    sin_s = jnp.tile(sin_s, (nb, 1))
    bd = _same_head_matrix()
    nsel_p = seq // SLC_BLOCK
    poolt_p = _pool_matrix(seq // CMP_STRIDE, nsel_p).T
    code_p = _block_code_table(seq)
    code_s = jnp.zeros((nb * tnew, LANES), F32)
    pool_s = _pool_matrix(past // CMP_STRIDE, past // SLC_BLOCK)
    expand_s = _expand_matrix(past // SLC_BLOCK, past)

    cache_t = _feature_major(cache_nsa_kv)
    winc_t = _feature_major(cache_win_kv)
    zero_state = jnp.zeros((batch, HG_HEADS, HG_DK, HG_DK), F32)
    wo_all, wu_all, wd_all = w_out.astype(BF16), w_up.astype(BF16), w_down.astype(BF16)

    tq = 256
    hg_tile = 128
    xp = x_prompt.reshape(batch * seq, D_MODEL)
    xs = x_sample.reshape(nb * tnew, D_MODEL)
    p_nsa, p_win, p_hg, s_nsa, s_win, s_hg = [], [], [], [], [], []
    for l in range(depth):
        pk = _pack_layer(l, attn_norm_g, w_in, q_norm_g, k_norm_g, cmp_pos, cmp_w1, cmp_b1, cmp_w2,
                         hg_norm_g, mlp_norm_g)
        lb = lb_all[l][None, :]

        qc, qr, nsa, win, kva, gates, hgin = _proj(xp, pk["g1"], pk["w_pack"], cos_p, sin_p, code_p, pk["gq"], pk["gk"], bd, 512)
        cmp = _compress_prompt(nsa, batch, seq, pk["cw"] + (bd,))
        o_nsa = _attn_prompt(qc, qr, gates, cmp, kva, poolt_p, batch, seq, tq, 512)
        o_hg, hfin = _hgrn(hgin.reshape(batch, seq, 4 * HG_WIDTH), lb, pk["hg_gain"], zero_state, hg_tile, batch, BF16)
        xp = _mlp(xp, o_nsa, o_hg.reshape(batch * seq, HG_WIDTH), wo_all, pk["g2"], wu_all, wd_all, l, 512)
        p_nsa.append(nsa.reshape(batch, seq, 4, KV_HEADS, HEAD_DIM))
        nkeep = min(WINDOW, seq)
        p_win.append(win.reshape(batch, seq, 2 * KV_WIDTH)[:, seq - nkeep:].reshape(batch, nkeep, 2, KV_HEADS, HEAD_DIM))
        p_hg.append(hfin)

        ns = nb * tnew
        qc, qr, nsa, win, _, gates, hgin = _proj(xs, pk["g1"], pk["w_pack"], cos_s, sin_s, code_s, pk["gq"], pk["gk"], bd, ns)
        cmp = _compress_sample(page_table, cache_t, pk["cw"] + (bd,), l)
        r3 = lambda a: a.reshape(nb, tnew, a.shape[-1])
        o_nsa = _attn_sample(page_table, cache_t, r3(qc), r3(qr), r3(gates), cmp, r3(nsa), r3(win), winc_t,
                             pool_s, expand_s, l, tnew)
        o_hg, hfin = _hgrn(r3(hgin), lb, pk["hg_gain"], state_hgrn[l], hg_tile, 4, F32)
        xs = _mlp(xs, o_nsa.reshape(ns, NSA_WIDTH).astype(BF16), o_hg.reshape(ns, HG_WIDTH).astype(BF16),
                  wo_all, pk["g2"], wu_all, wd_all, l, ns)
        s_nsa.append(nsa.reshape(nb, tnew, 4, KV_HEADS, HEAD_DIM))
        s_win.append(win.reshape(nb, tnew, 2, KV_HEADS, HEAD_DIM))
        s_hg.append(hfin)

    return (xp.reshape(batch, seq, D_MODEL), xs.reshape(nb, tnew, D_MODEL),
            jnp.stack(p_nsa), jnp.stack(p_win), jnp.stack(p_hg),
            jnp.stack(s_nsa), jnp.stack(s_win), jnp.stack(s_hg))
```

```python
import functools
import math

import numpy as np
import jax
import jax.numpy as jnp
from jax import lax
from jax.experimental import pallas as pl
from jax.experimental.pallas import tpu as pltpu

F32 = jnp.float32
BF16 = jnp.bfloat16

D_MODEL = 1024
HEAD_DIM = 64
NSA_HEADS = 8
KV_HEADS = 2
GROUP = NSA_HEADS // KV_HEADS
NSA_WIDTH = NSA_HEADS * HEAD_DIM
KV_WIDTH = KV_HEADS * HEAD_DIM
CMP_STRIDE = 16
CMP_HIDDEN = 4 * HEAD_DIM
SLC_BLOCK = 64
SLC_TOP_N = 16
WINDOW = 512
FORCED_SCORE = 1.0e4
HG_HEADS = 4
HG_DK = 128
HG_WIDTH = HG_HEADS * HG_DK
HG_CHUNK = 32
D_FF = 4 * D_MODEL
ROPE_THETA = 10000.0
EPS = 1e-6
SCALE = HEAD_DIM ** -0.5
LOG2E = 1.4426950408889634
RANK_UNROLL = 4
NEG = -1e30
BLOCK_CODE = -(2.0 ** 100)

LANES = 128
PA_W = NSA_WIDTH + 6 * KV_WIDTH + LANES
COL_KC, COL_VC, COL_KS, COL_VS, COL_KW, COL_VW = (NSA_WIDTH + i * KV_WIDTH for i in range(6))
COL_GATE = NSA_WIDTH + 6 * KV_WIDTH
VMEM_LIMIT = 56 * 1024 * 1024


def _cparams(sem):
    return pltpu.CompilerParams(dimension_semantics=sem, vmem_limit_bytes=VMEM_LIMIT)


def _dot(a, b):
    return jnp.dot(a, b, preferred_element_type=F32)


def _dot_nt(a, b):
    return lax.dot_general(a, b, (((1,), (1,)), ((), ())), preferred_element_type=F32)


def _split_dot(a, b, nt=False):
    hi = a.astype(BF16)
    lo = (a - hi.astype(F32)).astype(BF16)
    f = _dot_nt if nt else _dot
    return f(hi, b) + f(lo, b)


def _head_norm(v, bd, gain):
    ss = _split_dot(v * v, bd)
    return v * lax.rsqrt(ss * (1.0 / HEAD_DIM) + EPS) * gain


def _rope(v, cos, sin, first_half):
    sw = jnp.where(first_half, pltpu.roll(v, LANES - HEAD_DIM // 2, 1), pltpu.roll(v, HEAD_DIM // 2, 1))
    return v * cos + sw * sin


def _sigmoid(x):
    return 1.0 / (1.0 + jnp.exp(-x))


def _proj_body(x_ref, g_ref, w_ref, cos_ref, sin_ref, code_ref, gq_ref, gk_ref, bd_ref,
               qc_ref, qr_ref, nsa_ref, win_ref, kva_ref, gate_ref, hgin_ref):
    x = x_ref[...]
    ms = jnp.mean(x * x, axis=-1, keepdims=True)
    h = (x * lax.rsqrt(ms + EPS) * g_ref[...]).astype(BF16)
    pa = _dot(h, w_ref[:, :PA_W])
    hgin_ref[...] = _dot(h, w_ref[:, PA_W:])
    cos = cos_ref[...]
    sin = sin_ref[...]
    bd = bd_ref[...]
    lane = lax.broadcasted_iota(jnp.int32, cos.shape, 1)
    first_half = (lane % HEAD_DIM) < (HEAD_DIM // 2)
    gq = gq_ref[...]
    for c in range(NSA_WIDTH // LANES):
        sl = slice(LANES * c, LANES * (c + 1))
        qn = _head_norm(pa[:, sl], bd, gq) * (SCALE * LOG2E)
        qc_ref[:, sl] = qn.astype(BF16)
        qr_ref[:, sl] = _rope(qn, cos, sin, first_half).astype(BF16)
    kc = pa[:, COL_KC:COL_KC + KV_WIDTH]
    vc = pa[:, COL_VC:COL_VC + KV_WIDTH]
    ks = _rope(_head_norm(pa[:, COL_KS:COL_KS + KV_WIDTH], bd, gk_ref[0:1, :]), cos, sin, first_half)
    vs = pa[:, COL_VS:COL_VS + KV_WIDTH]
    kw = _rope(_head_norm(pa[:, COL_KW:COL_KW + KV_WIDTH], bd, gk_ref[1:2, :]), cos, sin, first_half)
    vw = pa[:, COL_VW:COL_VW + KV_WIDTH]
    for i, a in enumerate((kc, vc, ks, vs)):
        nsa_ref[:, LANES * i:LANES * (i + 1)] = a
    win_ref[:, 0:LANES] = kw
    win_ref[:, LANES:2 * LANES] = vw
    low = lane < HEAD_DIM
    code = code_ref[...]
    aug = []
    for kk, vv, fill in ((ks, vs, code), (kw, vw, 0.0)):
        aug += [jnp.where(low, kk, fill), jnp.where(low, pltpu.roll(kk, HEAD_DIM, 1), fill),
                jnp.where(low, vv, 1.0), jnp.where(low, pltpu.roll(vv, HEAD_DIM, 1), 1.0)]
    for i, a in enumerate(aug):
        kva_ref[:, LANES * i:LANES * (i + 1)] = a.astype(BF16)
    gate_ref[...] = _sigmoid(pa[:, COL_GATE:COL_GATE + LANES])


def _proj(x, g, w, cos, sin, code, gq, gk, bd, tn):
    n = x.shape[0]
    nt = cos.shape[0] // tn
    wtot = w.shape[1]
    row = lambda i: (i, 0)
    const = lambda i: (0, 0)
    outs = [
        jax.ShapeDtypeStruct((n, NSA_WIDTH), BF16),
        jax.ShapeDtypeStruct((n, NSA_WIDTH), BF16),
        jax.ShapeDtypeStruct((n, 4 * KV_WIDTH), F32),
        jax.ShapeDtypeStruct((n, 2 * KV_WIDTH), F32),
        jax.ShapeDtypeStruct((n, 8 * LANES), BF16),
        jax.ShapeDtypeStruct((n, LANES), F32),
        jax.ShapeDtypeStruct((n, 4 * HG_WIDTH), F32),
    ]
    return pl.pallas_call(
        _proj_body,
        grid=(n // tn,),
        in_specs=[
            pl.BlockSpec((tn, D_MODEL), row),
            pl.BlockSpec((1, D_MODEL), const),
            pl.BlockSpec((D_MODEL, wtot), const),
            pl.BlockSpec((tn, LANES), lambda i: (i % nt, 0)),
            pl.BlockSpec((tn, LANES), lambda i: (i % nt, 0)),
            pl.BlockSpec((tn, LANES), lambda i: (i % nt, 0)),
            pl.BlockSpec((1, LANES), const),
            pl.BlockSpec((2, LANES), const),
            pl.BlockSpec((LANES, LANES), const),
        ],
        out_specs=[pl.BlockSpec((tn, o.shape[1]), row) for o in outs],
        out_shape=outs,
        compiler_params=_cparams(("parallel",)),
        name="proj",
    )(x, g, w, cos, sin, code, gq, gk, bd)


def _gelu_tanh(x):
    return 0.5 * x * (1.0 + jnp.tanh(0.7978845608028654 * (x + 0.044715 * (x * x * x))))


def _fill_flat(load_rows, flat_ref, nch):
    for j in range(CMP_STRIDE):
        for slot in range(2):
            rows = load_rows(j, slot)
            for g in range(KV_HEADS):
                flat_ref[2 * slot + g, :, HEAD_DIM * j:HEAD_DIM * (j + 1)] = rows[:, HEAD_DIM * g:HEAD_DIM * (g + 1)]


def _compress_core(flat_ref, nch, nvalid, pe_ref, w1_ref, b1_ref, w2_ref, kn_ref, bd_ref, out_ref):
    rowid = lax.broadcasted_iota(jnp.int32, (nch, LANES), 0)
    low = lax.broadcasted_iota(jnp.int32, (nch, LANES), 1) < HEAD_DIM
    for slot in range(2):
        w1 = w1_ref[slot]
        pe = pe_ref[slot]
        pw = _split_dot(pe, w1)
        const = pw[0:1, :CMP_HIDDEN] + pw[8:9, CMP_HIDDEN:] + b1_ref[slot]
        for g in range(KV_HEADS):
            f = flat_ref[2 * slot + g].astype(BF16)
            ab = _dot(f, w1)
            pre = ab[:, :CMP_HIDDEN] + pltpu.roll(ab[:, CMP_HIDDEN:], nch - 1, 0) + const
            hid = _gelu_tanh(pre).astype(BF16)
            o = _dot(hid, w2_ref[slot])
            if slot == 0:
                o = _head_norm(o, bd_ref[...], kn_ref[...])
            else:
                o = jnp.where(low, o, 1.0)
            out_ref[2 * slot + g] = jnp.where(rowid < nvalid, o, 0.0).astype(BF16)


def _compress_prompt_body(xk_ref, xv_ref, pe_ref, w1_ref, b1_ref, w2_ref, kn_ref, bd_ref, out_ref, flat_ref):
    nch = flat_ref.shape[1]
    _fill_flat(lambda j, slot: (xk_ref, xv_ref)[slot][pl.ds(j, nch, stride=CMP_STRIDE), :], flat_ref, nch)
    _compress_core(flat_ref, nch, nch - 1, pe_ref, w1_ref, b1_ref, w2_ref, kn_ref, bd_ref, out_ref)


def _cmp_weight_specs(nidx):
    z = lambda *a: (0,) * nidx
    return [
        pl.BlockSpec((2, 16, CMP_STRIDE * HEAD_DIM), lambda *a: (0, 0, 0)),
        pl.BlockSpec((2, CMP_STRIDE * HEAD_DIM, 2 * CMP_HIDDEN), lambda *a: (0, 0, 0)),
        pl.BlockSpec((2, 1, CMP_HIDDEN), lambda *a: (0, 0, 0)),
        pl.BlockSpec((2, CMP_HIDDEN, LANES), lambda *a: (0, 0, 0)),
        pl.BlockSpec((1, LANES), lambda *a: (0, 0)),
        pl.BlockSpec((LANES, LANES), lambda *a: (0, 0)),
    ]


def _compress_prompt(nsa, batch, seq, cw):
    nch = seq // CMP_STRIDE
    return pl.pallas_call(
        _compress_prompt_body,
        grid=(batch,),
        in_specs=[pl.BlockSpec((seq, KV_WIDTH), lambda b: (b, 0)),
                  pl.BlockSpec((seq, KV_WIDTH), lambda b: (b, 1))] + _cmp_weight_specs(1),
        out_specs=pl.BlockSpec((None, 4, nch, LANES), lambda b: (b, 0, 0, 0)),
        out_shape=jax.ShapeDtypeStruct((batch, 4, nch, LANES), BF16),
        scratch_shapes=[pltpu.VMEM((4, nch, CMP_STRIDE * HEAD_DIM), F32)],
        compiler_params=_cparams(("parallel",)),
        name="compress_prompt",
    )(nsa, nsa, *cw)


def _softmax_masked(s, mask):
    sm = jnp.where(mask, s, NEG)
    m = jnp.max(sm, axis=-1, keepdims=True)
    e = jnp.where(mask, jnp.exp2(sm - m), 0.0)
    return e / jnp.maximum(jnp.sum(e, axis=-1, keepdims=True), 1e-30)


def _stack_heads(q, g, rows_f32=False):
    parts = [q[:, HEAD_DIM * (GROUP * g + n):HEAD_DIM * (GROUP * g + n + 1)] for n in range(GROUP)]
    return jnp.concatenate(parts, axis=0)


def _place_heads(q, g, fill, low):
    parts = []
    for n in range(GROUP):
        hd = GROUP * g + n
        chunk = q[:, LANES * (hd // 2):LANES * (hd // 2 + 1)].astype(F32)
        if hd % 2:
            chunk = pltpu.roll(chunk, HEAD_DIM, 1)
        parts.append(jnp.where(low, chunk, fill).astype(BF16))
    return jnp.concatenate(parts, axis=0)


def _attend(q4, k, v, bias):
    s = _add_bias(_dot_nt(q4, k), bias)
    e = jnp.exp2(s - jnp.max(s, axis=-1, keepdims=True))
    return _dot(e.astype(BF16), v), e


def _add_bias(s, bias):
    if bias is None:
        return s
    tq, width = bias.shape
    return (s.reshape(GROUP, tq, width) + bias[None]).reshape(GROUP * tq, width)


def _attn_prompt_body(qc_ref, qr_ref, gate_ref, cmp_ref, slc_ref, win_ref, poolt_ref, o_ref, score_ref,
                      *, tq, seq, kchunk):
    i = pl.program_id(1)
    q0 = i * tq
    ncmp = cmp_ref.shape[1]
    nsel = seq // SLC_BLOCK
    m4 = GROUP * tq
    qc = qc_ref[...]
    qr = qr_ref[...]
    gates = gate_ref[...]
    low = lax.broadcasted_iota(jnp.int32, (tq, LANES), 1) < HEAD_DIM
    qpos = q0 + lax.broadcasted_iota(jnp.int32, (tq, 1), 0)
    rep4 = lambda a: jnp.concatenate([a] * GROUP, axis=0)

    c_end = lax.broadcasted_iota(jnp.int32, (1, ncmp), 1) * CMP_STRIDE + (2 * CMP_STRIDE - 1)
    cmp_bias = jnp.where(c_end <= qpos, 0.0, NEG)
    cmp_any = rep4(jnp.where(qpos >= 2 * CMP_STRIDE - 1, 1.0, 0.0))
    wlen = WINDOW + tq
    w0 = pl.multiple_of(jnp.maximum(q0 - WINDOW, 0), tq)
    diff = qpos - (w0 + lax.broadcasted_iota(jnp.int32, (1, wlen), 1))
    win_bias = jnp.where((diff >= 0) & (diff < WINDOW), 0.0, NEG)
    nk = (q0 + tq + kchunk - 1) // kchunk
    kpos_d = (nk - 1) * kchunk + lax.broadcasted_iota(jnp.int32, (1, kchunk), 1)
    diag_bias = jnp.where(kpos_d <= qpos, 0.0, NEG)

    blk = lax.broadcasted_iota(jnp.int32, (nsel, tq), 0)
    qp_l = q0 + lax.broadcasted_iota(jnp.int32, (nsel, tq), 1)
    cur = qp_l // SLC_BLOCK
    forced = (blk == 0) | (blk == cur) | (blk == cur - 1)
    visible = blk * SLC_BLOCK <= qp_l
    nvis = jnp.minimum((q0 + tq + SLC_BLOCK - 1) // SLC_BLOCK, nsel)

    kcol = lambda g: slice(LANES * g, LANES * (g + 1))
    vcol = lambda g: slice(LANES * (KV_HEADS + g), LANES * (KV_HEADS + g + 1))

    for g in range(KV_HEADS):
        acc_c, e_c = _attend(_place_heads(qc, g, 0.0, low), cmp_ref[g], cmp_ref[KV_HEADS + g], cmp_bias)
        inv_c = cmp_any / acc_c[:, HEAD_DIM:HEAD_DIM + 1]
        o_cmp = acc_c * inv_c
        p = e_c * inv_c
        psum = p[0:tq] + p[tq:2 * tq] + p[2 * tq:3 * tq] + p[3 * tq:4 * tq]

        p_hi = psum.astype(BF16)
        p_lo = (psum - p_hi.astype(F32)).astype(BF16)
        imp_t = _dot_nt(poolt_ref[...], p_hi) + _dot_nt(poolt_ref[...], p_lo)
        score = jnp.where(visible, jnp.where(forced, FORCED_SCORE, imp_t), -1.0)
        score_ref[...] = score

        def rank_step(kk, rank):
            for u in range(RANK_UNROLL):
                k = kk * RANK_UNROLL + u
                rk = score_ref[pl.ds(k, 1), :]
                rank = rank + jnp.where((rk > score) | ((rk == score) & (blk > k)), 1.0, 0.0)
            return rank

        rank = lax.fori_loop(0, (nvis + RANK_UNROLL - 1) // RANK_UNROLL, rank_step, jnp.zeros((nsel, tq), F32))
        notsel = jnp.where(visible & (rank < float(SLC_TOP_N)), 0.0, 1.0)
        pieces = [jnp.zeros((HEAD_DIM, tq), F32), notsel]
        if nsel < HEAD_DIM:
            pieces.append(jnp.ones((HEAD_DIM - nsel, tq), F32))
        fill = jnp.concatenate(pieces, axis=0).T

        q4r = _place_heads(qr, g, fill, low)

        def absorb(c, m_i, acc, bias):
            k0 = pl.multiple_of(c * kchunk, kchunk)
            s = _add_bias(_dot_nt(q4r, slc_ref[pl.ds(k0, kchunk), kcol(g)]), bias)
            m_new = jnp.maximum(m_i, jnp.max(s, axis=-1, keepdims=True))
            e = jnp.exp2(s - m_new).astype(BF16)
            return m_new, jnp.exp2(m_i - m_new) * acc + _dot(e, slc_ref[pl.ds(k0, kchunk), vcol(g)])

        st = (jnp.full((m4, 1), NEG, F32), jnp.zeros((m4, LANES), F32))
        m_l, acc_l = lax.fori_loop(0, nk - 1, lambda c, cr: absorb(c, cr[0], cr[1], None), st)
        _, acc_s = absorb(nk - 1, m_l, acc_l, diag_bias)
        o_slc = acc_s / acc_s[:, HEAD_DIM:HEAD_DIM + 1]

        acc_w, _ = _attend(q4r, win_ref[pl.ds(w0, wlen), kcol(g)], win_ref[pl.ds(w0, wlen), vcol(g)], win_bias)
        o_win = acc_w / acc_w[:, HEAD_DIM:HEAD_DIM + 1]

        comb = []
        for n in range(GROUP):
            hd = GROUP * g + n
            r = slice(n * tq, (n + 1) * tq)
            comb.append(o_cmp[r] * gates[:, 3 * hd:3 * hd + 1] + o_slc[r] * gates[:, 3 * hd + 1:3 * hd + 2]
                        + o_win[r] * gates[:, 3 * hd + 2:3 * hd + 3])
        for pair in range(GROUP // 2):
            both = jnp.where(low, comb[2 * pair], pltpu.roll(comb[2 * pair + 1], HEAD_DIM, 1))
            c0 = LANES * ((GROUP // 2) * g + pair)
            o_ref[:, c0:c0 + LANES] = both.astype(BF16)


def _attn_prompt(qc, qr, gates, cmp, kva, poolt, batch, seq, tq, kchunk):
    n = batch * seq
    nq = seq // tq
    ncmp = cmp.shape[2]
    nsel = seq // SLC_BLOCK
    assert nsel <= HEAD_DIM, "block codes use the 64 spare contraction lanes"
    body = functools.partial(_attn_prompt_body, tq=tq, seq=seq, kchunk=kchunk)
    return pl.pallas_call(
        body,
        grid=(batch, nq),
        in_specs=[
            pl.BlockSpec((tq, NSA_WIDTH), lambda b, i: (b * nq + i, 0)),
            pl.BlockSpec((tq, NSA_WIDTH), lambda b, i: (b * nq + i, 0)),
            pl.BlockSpec((tq, LANES), lambda b, i: (b * nq + i, 0)),
            pl.BlockSpec((None, 4, ncmp, LANES), lambda b, i: (b, 0, 0, 0)),
            pl.BlockSpec((seq, 4 * LANES), lambda b, i: (b, 0)),
            pl.BlockSpec((seq, 4 * LANES), lambda b, i: (b, 1)),
            pl.BlockSpec((nsel, ncmp), lambda b, i: (0, 0)),
        ],
        out_specs=pl.BlockSpec((tq, NSA_WIDTH), lambda b, i: (b * nq + i, 0)),
        out_shape=jax.ShapeDtypeStruct((n, NSA_WIDTH), BF16),
        scratch_shapes=[pltpu.VMEM((nsel, tq), F32)],
        compiler_params=_cparams(("parallel", "parallel")),
        name="attn_prompt",
    )(qc, qr, gates, cmp, kva, kva, poolt)


def _hgrn_body(x_ref, lb_ref, gain_ref, s0_ref, o_ref, sfin_ref, st_ref, *, tile, nvalid, bstep):
    i = pl.program_id(1)
    nsub = tile // HG_CHUNK

    @pl.when(i == 0)
    def _():
        for bb in range(bstep):
            for hd in range(HG_HEADS):
                st_ref[bb * HG_HEADS + hd] = s0_ref[bb, hd].T

    for bb in range(bstep):
        _hgrn_tile(x_ref.at[bb], lb_ref, gain_ref, o_ref.at[bb], st_ref, bb * HG_HEADS, tile, nvalid, nsub)

    @pl.when(i == pl.num_programs(1) - 1)
    def _():
        for bb in range(bstep):
            for hd in range(HG_HEADS):
                sfin_ref[bb, hd] = st_ref[bb * HG_HEADS + hd].T


def _hgrn_tile(x_ref, lb_ref, gain_ref, o_ref, st_ref, st0, tile, nvalid, nsub):
    if nvalid < tile:
        x_all = jnp.concatenate([x_ref[...], jnp.zeros((tile - nvalid, 4 * HG_WIDTH), F32)], axis=0)
    row = lax.broadcasted_iota(jnp.int32, (tile, HG_DK), 0)
    rowc = row % HG_CHUNK
    col = lax.broadcasted_iota(jnp.int32, (tile, tile), 1)
    rr = lax.broadcasted_iota(jnp.int32, (tile, tile), 0)
    intra_mask = (rr // HG_CHUNK == col // HG_CHUNK) & (rr >= col)
    lane_t = lax.broadcasted_iota(jnp.int32, (HG_DK, tile), 1)
    valid = row < nvalid

    for hd in range(HG_HEADS):
        sl = lambda k: slice(k * HG_WIDTH + hd * HG_DK, k * HG_WIDTH + (hd + 1) * HG_DK)
        src = x_ref if nvalid == tile else x_all
        q = src[:, sl(0)]
        z = src[:, sl(1)]
        v = src[:, sl(2)]
        gt = src[:, sl(3)]
        lb = lb_ref[:, hd * HG_DK:(hd + 1) * HG_DK]
        log_sig = jnp.minimum(z, 0.0) - jnp.log(1.0 + jnp.exp(-jnp.abs(z)))
        t1 = jnp.log(1.0 - lb) + log_sig
        a = jnp.broadcast_to(jnp.log(lb), t1.shape)
        log_f = jnp.maximum(a, t1) + jnp.log(1.0 + jnp.exp(-jnp.abs(a - t1)))
        key = (1.0 - lb) * _sigmoid(-z)
        if nvalid < tile:
            log_f = jnp.where(valid, log_f, 0.0)
            key = jnp.where(valid, key, 0.0)
            v = jnp.where(valid, v, 0.0)
        qs = q * _sigmoid(q)
        b = log_f
        step = 1
        while step < HG_CHUNK:
            b = b + jnp.where(rowc >= step, pltpu.roll(b, step, 0), 0.0)
            step *= 2
        qe = (qs * jnp.exp(b)).astype(BF16)
        ke = (key * jnp.exp(-b)).astype(BF16)
        vb = v.astype(BF16)
        a_mat = jnp.where(intra_mask, _dot_nt(qe, ke), 0.0)
        o_acc = _dot(a_mat.astype(BF16), vb)
        vt = v.T
        b_lasts = [b[(c + 1) * HG_CHUNK - 1:(c + 1) * HG_CHUNK, :] for c in range(nsub)]
        b_end = jnp.broadcast_to(b_lasts[0], b.shape)
        for c in range(1, nsub):
            b_end = jnp.where(row >= c * HG_CHUNK, b_lasts[c], b_end)
        kd = (key * jnp.exp(b_end - b)).astype(BF16)
        st = st_ref[st0 + hd]
        o_parts = []
        for c in range(nsub):
            r = slice(c * HG_CHUNK, (c + 1) * HG_CHUNK)
            o_parts.append(_dot_nt(qe[r], st.astype(BF16)))
            vtc = jnp.where(lane_t // HG_CHUNK == c, vt, 0.0).astype(BF16)
            st = st * jnp.exp(b_lasts[c]) + _dot(vtc, kd)
        st_ref[st0 + hd] = st
        o = o_acc + jnp.concatenate(o_parts, axis=0)
        ms = jnp.mean(o * o, axis=-1, keepdims=True)
        o = o * lax.rsqrt(ms + EPS) * gain_ref[...] * _sigmoid(gt)
        o_ref[:, hd * HG_DK:(hd + 1) * HG_DK] = o[:nvalid].astype(o_ref.dtype)


def _hgrn(hgin, lb, gain, s0, tile, bstep, out_dtype):
    batch, rows, _ = hgin.shape
    nvalid = min(rows, tile)
    nt = rows // nvalid
    body = functools.partial(_hgrn_body, tile=tile, nvalid=nvalid, bstep=bstep)
    s_spec = pl.BlockSpec((bstep, HG_HEADS, HG_DK, HG_DK), lambda b, i: (b, 0, 0, 0))
    return pl.pallas_call(
        body,
        grid=(batch // bstep, nt),
        in_specs=[
            pl.BlockSpec((bstep, nvalid, 4 * HG_WIDTH), lambda b, i: (b, i, 0)),
            pl.BlockSpec((1, HG_WIDTH), lambda b, i: (0, 0)),
            pl.BlockSpec((1, HG_DK), lambda b, i: (0, 0)),
            s_spec,
        ],
        out_specs=[pl.BlockSpec((bstep, nvalid, HG_WIDTH), lambda b, i: (b, i, 0)), s_spec],
        out_shape=[jax.ShapeDtypeStruct((batch, rows, HG_WIDTH), out_dtype),
                   jax.ShapeDtypeStruct((batch, HG_HEADS, HG_DK, HG_DK), F32)],
        scratch_shapes=[pltpu.VMEM((bstep * HG_HEADS, HG_DK, HG_DK), F32)],
        compiler_params=_cparams(("parallel", "arbitrary")),
        name="hgrn",
    )(hgin, lb, gain, s0)


def _mlp_body(x_ref, on_ref, oh_ref, wo_ref, g_ref, wu_ref, wd_ref, y_ref, *, fchunk):
    x2 = x_ref[...] + _dot(jnp.concatenate([on_ref[...], oh_ref[...]], axis=1), wo_ref[...])
    ms = jnp.mean(x2 * x2, axis=-1, keepdims=True)
    h2 = (x2 * lax.rsqrt(ms + EPS) * g_ref[...]).astype(BF16)
    acc = x2
    for c in range(D_FF // fchunk):
        u = _dot(h2, wu_ref[:, c * fchunk:(c + 1) * fchunk])
        u = jnp.maximum(u, 0.0)
        acc = acc + _dot((u * u).astype(BF16), wd_ref[c * fchunk:(c + 1) * fchunk, :])
    y_ref[...] = acc


def _mlp(x, on, oh, wo, g, wu, wd, layer, tn, fchunk=1024):
    n = x.shape[0]
    row = lambda i: (i, 0)
    const = lambda i: (0, 0)
    slab = lambda i: (layer, 0, 0)
    return pl.pallas_call(
        functools.partial(_mlp_body, fchunk=fchunk),
        grid=(n // tn,),
        in_specs=[
            pl.BlockSpec((tn, D_MODEL), row),
            pl.BlockSpec((tn, NSA_WIDTH), row),
            pl.BlockSpec((tn, HG_WIDTH), row),
            pl.BlockSpec((None, D_MODEL, D_MODEL), slab),
            pl.BlockSpec((1, D_MODEL), const),
            pl.BlockSpec((None, D_MODEL, D_FF), slab),
            pl.BlockSpec((None, D_FF, D_MODEL), slab),
        ],
        out_specs=pl.BlockSpec((tn, D_MODEL), row),
        out_shape=jax.ShapeDtypeStruct((n, D_MODEL), F32),
        compiler_params=_cparams(("parallel",)),
        name="mlp",
    )(x, on, oh, wo, g, wu, wd)


def _pages_pipeline(page_copy, npages):
    b = pl.program_id(0)
    nb = pl.num_programs(0)
    slot = b % 2

    def start(bb, sl):
        def go(p, c):
            page_copy(bb, p, sl).start()
            return c
        lax.fori_loop(0, npages, go, 0, unroll=8)

    @pl.when(b == 0)
    def _():
        start(0, 0)

    @pl.when(b + 1 < nb)
    def _():
        start(b + 1, 1 - slot)

    def wait(p, c):
        page_copy(b, p, slot).wait()
        return c
    lax.fori_loop(0, npages, wait, 0, unroll=8)
    return slot


def _compress_sample_body(pt_ref, cache_ref, pe_ref, w1_ref, b1_ref, w2_ref, kn_ref, bd_ref,
                          out_ref, buf_ref, rows_ref, flat_ref, sem_ref, *, layer, page, npages):
    nfeat = 2 * KV_WIDTH

    def page_copy(b, p, slot):
        return pltpu.make_async_copy(cache_ref.at[layer, pt_ref[b, p], pl.ds(0, nfeat), :],
                                     buf_ref.at[slot, p], sem_ref.at[slot])

    slot = _pages_pipeline(page_copy, npages)
    eye = (lax.broadcasted_iota(jnp.int32, (page, page), 0)
           == lax.broadcasted_iota(jnp.int32, (page, page), 1)).astype(BF16)

    def to_rows(p, c):
        x = _dot_nt(eye, buf_ref[slot, p].astype(BF16))
        r0 = pl.multiple_of(p * page, page)
        rows_ref[0, pl.ds(r0, page), :] = x[:, :KV_WIDTH]
        rows_ref[1, pl.ds(r0, page), :] = x[:, KV_WIDTH:]
        return c
    lax.fori_loop(0, npages, to_rows, 0, unroll=8)
    nch = flat_ref.shape[1]
    _fill_flat(lambda j, s: rows_ref[s, pl.ds(j, nch, stride=CMP_STRIDE), :], flat_ref, nch)
    _compress_core(flat_ref, nch, nch - 1, pe_ref, w1_ref, b1_ref, w2_ref, kn_ref, bd_ref, out_ref)


def _compress_sample(page_table, cache_t, cw, layer):
    nb, npages = page_table.shape
    page = cache_t.shape[3]
    past = npages * page
    nch = past // CMP_STRIDE
    body = functools.partial(_compress_sample_body, layer=layer, page=page, npages=npages)
    return pl.pallas_call(
        body,
        grid_spec=pltpu.PrefetchScalarGridSpec(
            num_scalar_prefetch=1,
            grid=(nb,),
            in_specs=[pl.BlockSpec(memory_space=pl.ANY)] + _cmp_weight_specs(2),
            out_specs=pl.BlockSpec((None, 4, nch, LANES), lambda b, pt: (b, 0, 0, 0)),
            scratch_shapes=[
                pltpu.VMEM((2, npages, 2 * KV_WIDTH, page), F32),
                pltpu.VMEM((2, past, KV_WIDTH), F32),
                pltpu.VMEM((4, nch, CMP_STRIDE * HEAD_DIM), F32),
                pltpu.SemaphoreType.DMA((2,)),
            ],
        ),
        out_shape=jax.ShapeDtypeStruct((nb, 4, nch, LANES), BF16),
        compiler_params=_cparams(("arbitrary",)),
        name="compress_sample",
    )(page_table, cache_t, *cw)


def _two_part_attention(s1, mask1, v1t, s2, mask2, v2):
    s1 = jnp.where(mask1, s1, NEG)
    s2 = jnp.where(mask2, s2, NEG)
    m = jnp.maximum(jnp.max(s1, axis=-1, keepdims=True), jnp.max(s2, axis=-1, keepdims=True))
    e1 = jnp.where(mask1, jnp.exp2(s1 - m), 0.0)
    e2 = jnp.where(mask2, jnp.exp2(s2 - m), 0.0)
    l = jnp.sum(e1, axis=-1, keepdims=True) + jnp.sum(e2, axis=-1, keepdims=True)
    o = _dot_nt(e1.astype(BF16), v1t) + _dot(e2.astype(BF16), v2)
    return o / jnp.maximum(l, 1e-30)


def _attn_sample_body(pt_ref, cache_ref, qc_ref, qr_ref, gate_ref, cmp_ref, nsa_new_ref, win_new_ref, winc_ref,
                      pool_ref, e_ref, o_ref, buf_ref, sem_ref, *, layer, page, npages, tnew):
    nfeat = 2 * KV_WIDTH

    def page_copy(b, p, slot):
        return pltpu.make_async_copy(cache_ref.at[layer, pt_ref[b, p], pl.ds(nfeat, nfeat), :],
                                     buf_ref.at[slot, :, pl.ds(p * page, page)], sem_ref.at[slot])

    slot = _pages_pipeline(page_copy, npages)
    past = npages * page
    ncmp = cmp_ref.shape[1]
    nblk = past // SLC_BLOCK
    nlane = LANES * ((nblk + 1 + LANES - 1) // LANES)
    m4 = GROUP * tnew
    qc = qc_ref[...].astype(F32)
    qr = qr_ref[...].astype(F32)
    gates = gate_ref[...]
    nsa_new = nsa_new_ref[...]
    win_new = win_new_ref[...]
    pad = jnp.zeros((16 - tnew, HEAD_DIM), F32)
    new16 = lambda a, c0: jnp.concatenate([a[:, c0:c0 + HEAD_DIM], pad], axis=0).astype(BF16)
    t4 = lax.broadcasted_iota(jnp.int32, (m4, 1), 0) % tnew
    u16 = lax.broadcasted_iota(jnp.int32, (1, 16), 1)
    new_mask = (u16 <= t4) & (u16 < tnew)

    blk = lax.broadcasted_iota(jnp.int32, (tnew, nlane), 1)
    qpos = past + lax.broadcasted_iota(jnp.int32, (tnew, nlane), 0)
    cur = qpos // SLC_BLOCK
    forced = (blk == 0) | (blk == cur) | (blk == cur - 1)
    visible = (blk * SLC_BLOCK <= qpos) & (blk <= nblk)

    for g in range(KV_HEADS):
        hsl = slice(HEAD_DIM * g, HEAD_DIM * (g + 1))
        q4 = _stack_heads(qc, g).astype(BF16)
        s = _dot_nt(q4, cmp_ref[g][:, :HEAD_DIM])
        c_end = lax.broadcasted_iota(jnp.int32, (1, ncmp), 1) * CMP_STRIDE + (2 * CMP_STRIDE - 1)
        p = _softmax_masked(s, c_end <= past + t4)
        o_cmp = _dot(p.astype(BF16), cmp_ref[KV_HEADS + g][:, :HEAD_DIM])

        imp4 = _split_dot(p, pool_ref[...])
        imp = imp4[0:tnew]
        for n in range(1, GROUP):
            imp = imp + imp4[n * tnew:(n + 1) * tnew]
        if nlane > nblk:
            imp = jnp.concatenate([imp, jnp.zeros((tnew, nlane - nblk), F32)], axis=1)
        score = jnp.where(visible, jnp.where(forced, FORCED_SCORE, imp), -1.0)
        rank = jnp.zeros((tnew, nlane), F32)
        for k in range(nblk + 1):
            ck = score[:, k:k + 1]
            beats = (ck > score) | ((ck == score) & (blk > k))
            rank = rank + jnp.where(beats, 1.0, 0.0)
        sel = jnp.where(visible & (rank < float(SLC_TOP_N)), 1.0, 0.0)

        q4r = _stack_heads(qr, g).astype(BF16)
        ksp_t = buf_ref[slot, HEAD_DIM * g:HEAD_DIM * (g + 1), :].astype(BF16)
        vsp_t = buf_ref[slot, KV_WIDTH + HEAD_DIM * g:KV_WIDTH + HEAD_DIM * (g + 1), :].astype(BF16)
        sel4 = jnp.concatenate([sel[:, :nblk]] * GROUP, axis=0).astype(BF16)
        mpast = _dot(sel4, e_ref[...]) > 0.5
        ks_new = new16(nsa_new, 2 * KV_WIDTH + HEAD_DIM * g)
        vs_new = new16(nsa_new, 3 * KV_WIDTH + HEAD_DIM * g)
        cur_sel4 = jnp.concatenate([sel[:, nblk:nblk + 1]] * GROUP, axis=0) > 0.5
        o_slc = _two_part_attention(_dot(q4r, ksp_t), mpast, vsp_t,
                                    _dot_nt(q4r, ks_new), new_mask & cur_sel4, vs_new)

        nbuf = winc_ref.shape[1]
        kwp_t = winc_ref[HEAD_DIM * g:HEAD_DIM * (g + 1), :].astype(BF16)
        vwp_t = winc_ref[KV_WIDTH + HEAD_DIM * g:KV_WIDTH + HEAD_DIM * (g + 1), :].astype(BF16)
        kw_new = new16(win_new, HEAD_DIM * g)
        vw_new = new16(win_new, KV_WIDTH + HEAD_DIM * g)
        diff = (t4 + nbuf) - lax.broadcasted_iota(jnp.int32, (1, nbuf), 1)
        o_win = _two_part_attention(_dot(q4r, kwp_t), (diff >= 0) & (diff < WINDOW), vwp_t,
                                    _dot_nt(q4r, kw_new), new_mask, vw_new)

        for n in range(GROUP):
            hd = GROUP * g + n
            r = slice(n * tnew, (n + 1) * tnew)
            o = (o_cmp[r] * gates[:, 3 * hd:3 * hd + 1] + o_slc[r] * gates[:, 3 * hd + 1:3 * hd + 2]
                 + o_win[r] * gates[:, 3 * hd + 2:3 * hd + 3])
            o_ref[:, HEAD_DIM * hd:HEAD_DIM * (hd + 1)] = o


def _attn_sample(page_table, cache_t, qc, qr, gates, cmp, nsa_new, win_new, winc_t, pool, expand, layer, tnew):
    nb, npages = page_table.shape
    page = cache_t.shape[3]
    past = npages * page
    ncmp = cmp.shape[2]
    nblk = past // SLC_BLOCK
    nbuf = winc_t.shape[3]
    body = functools.partial(_attn_sample_body, layer=layer, page=page, npages=npages, tnew=tnew)
    per_b = lambda w: pl.BlockSpec((None, tnew, w), lambda b, pt: (b, 0, 0))
    return pl.pallas_call(
        body,
        grid_spec=pltpu.PrefetchScalarGridSpec(
            num_scalar_prefetch=1,
            grid=(nb,),
            in_specs=[
                pl.BlockSpec(memory_space=pl.ANY),
                per_b(NSA_WIDTH), per_b(NSA_WIDTH), per_b(LANES),
                pl.BlockSpec((None, 4, ncmp, LANES), lambda b, pt: (b, 0, 0, 0)),
                per_b(4 * KV_WIDTH), per_b(2 * KV_WIDTH),
                pl.BlockSpec((None, None, 2 * KV_WIDTH, nbuf), lambda b, pt: (layer, b, 0, 0)),
                pl.BlockSpec((ncmp, nblk), lambda b, pt: (0, 0)),
                pl.BlockSpec((nblk, past), lambda b, pt: (0, 0)),
            ],
            out_specs=per_b(NSA_WIDTH),
            scratch_shapes=[
                pltpu.VMEM((2, 2 * KV_WIDTH, past), F32),
                pltpu.SemaphoreType.DMA((2,)),
            ],
        ),
        out_shape=jax.ShapeDtypeStruct((nb, tnew, NSA_WIDTH), F32),
        compiler_params=_cparams(("arbitrary",)),
        name="attn_sample",
    )(page_table, cache_t, qc, qr, gates, cmp, nsa_new, win_new, winc_t, pool, expand)


def _rope_tables(pos):
    half = HEAD_DIM // 2
    inv_freq = ROPE_THETA ** (-jnp.arange(half, dtype=F32) / half)
    ang = pos.astype(F32)[:, None] * inv_freq[None, :]
    cos = jnp.cos(ang)
    sin = jnp.sin(ang)
    cos_t = jnp.tile(jnp.concatenate([cos, cos], axis=-1), (1, LANES // HEAD_DIM))
    sin_t = jnp.tile(jnp.concatenate([-sin, sin], axis=-1), (1, LANES // HEAD_DIM))
    return cos_t, sin_t


def _pack_layer(l, attn_norm_g, w_in, q_norm_g, k_norm_g, cmp_pos, cmp_w1, cmp_b1, cmp_w2,
                hg_norm_g, mlp_norm_g):
    w = w_in[l]
    gate_lo = COL_GATE
    gate_hi = COL_GATE + 3 * NSA_HEADS
    w_gate = jnp.pad(w[:, gate_lo:gate_hi], ((0, 0), (0, LANES - 3 * NSA_HEADS)))
    w_pack = jnp.concatenate([w[:, :gate_lo], w_gate, w[:, gate_hi:]], axis=1).astype(BF16)
    tile2 = lambda v: jnp.tile(v, LANES // HEAD_DIM)[None, :]
    half = CMP_STRIDE * HEAD_DIM
    w1 = cmp_w1[l]
    w1cat = jnp.concatenate([w1[:, :half], w1[:, half:]], axis=-1).astype(BF16)
    pe = cmp_pos[l].reshape(2, 2, 1, half)
    pe16 = jnp.broadcast_to(pe, (2, 2, 8, half)).reshape(2, 16, half)
    w2 = cmp_w2[l]
    w2pad = jnp.concatenate([w2, jnp.zeros_like(w2)], axis=-1).astype(BF16)
    return dict(
        g1=attn_norm_g[l][None, :], w_pack=w_pack, gq=tile2(q_norm_g[l]),
        gk=jnp.concatenate([tile2(k_norm_g[l, 1]), tile2(k_norm_g[l, 2])], axis=0),
        cw=(pe16, w1cat, cmp_b1[l][:, None, :], w2pad, tile2(k_norm_g[l, 0])),
        hg_gain=hg_norm_g[l][None, :], g2=mlp_norm_g[l][None, :],
    )


def _same_head_matrix():
    i = np.arange(LANES)
    return jnp.asarray((i[:, None] // HEAD_DIM) == (i[None, :] // HEAD_DIM), BF16)


def _pool_matrix(ncmp_rows, nblk):
    c = np.arange(ncmp_rows)
    j = np.arange(nblk)
    return jnp.asarray((c[:, None] // (SLC_BLOCK // CMP_STRIDE)) == j[None, :], BF16)


def _expand_matrix(nblk, nkeys):
    j = np.arange(nblk)
    s = np.arange(nkeys)
    return jnp.asarray(j[:, None] == (s[None, :] // SLC_BLOCK), BF16)


def _block_code_table(npos):
    lane = np.arange(LANES)[None, :]
    blk = (np.arange(npos) // SLC_BLOCK)[:, None]
    return jnp.asarray(np.where(lane == HEAD_DIM + blk, BLOCK_CODE, 0.0), F32)


def _feature_major(cache):
    lead = cache.shape[:-4]
    rows = cache.shape[-4]
    n = len(lead)
    perm = tuple(range(n)) + (n + 1, n + 2, n + 3, n)
    return jnp.transpose(cache, perm).reshape(*lead, -1, rows)


def kernel(x_prompt, x_sample, cache_nsa_kv, cache_win_kv, state_hgrn, page_table, attn_norm_g, w_in,
           q_norm_g, k_norm_g, cmp_pos, cmp_w1, cmp_b1, cmp_w2, hg_lb_param, hg_norm_g, w_out,
           mlp_norm_g, w_up, w_down):
    depth = w_in.shape[0]
    batch, seq, _ = x_prompt.shape
    nb, tnew, _ = x_sample.shape
    npool, page = cache_nsa_kv.shape[1], cache_nsa_kv.shape[2]
    npages = page_table.shape[1]
    past = npages * page
    nbuf = cache_win_kv.shape[2]

    lb_all = jnp.cumsum(jax.nn.softmax(hg_lb_param.astype(F32), axis=0), axis=0)
    lb_all = lb_all - lb_all[0]

    cos_p, sin_p = _rope_tables(jnp.arange(seq, dtype=jnp.int32))
    cos_s, sin_s = _rope_tables(past + jnp.arange(tnew, dtype=jnp.int32))
    cos_s = jnp.tile(cos_s, (nb, 1))
    sin_s = jnp.tile(sin_s, (nb, 1))
    bd = _same_head_matrix()
    nsel_p = seq // SLC_BLOCK
    poolt_p = _pool_matrix(seq // CMP_STRIDE, nsel_p).T
    code_p = _block_code_table(seq)
    code_s = jnp.zeros((nb * tnew, LANES), F32)
    pool_s = _pool_matrix(past // CMP_STRIDE, past // SLC_BLOCK)
    expand_s = _expand_matrix(past // SLC_BLOCK, past)

    cache_t = _feature_major(cache_nsa_kv)
    winc_t = _feature_major(cache_win_kv)
    zero_state = jnp.zeros((batch, HG_HEADS, HG_DK, HG_DK), F32)
    wo_all, wu_all, wd_all = w_out.astype(BF16), w_up.astype(BF16), w_down.astype(BF16)

    tq = 256
    hg_tile = 128
    xp = x_prompt.reshape(batch * seq, D_MODEL)
    xs = x_sample.reshape(nb * tnew, D_MODEL)
    p_nsa, p_win, p_hg, s_nsa, s_win, s_hg = [], [], [], [], [], []
    for l in range(depth):
        pk = _pack_layer(l, attn_norm_g, w_in, q_norm_g, k_norm_g, cmp_pos, cmp_w1, cmp_b1, cmp_w2,
                         hg_norm_g, mlp_norm_g)
        lb = lb_all[l][None, :]

        qc, qr, nsa, win, kva, gates, hgin = _proj(xp, pk["g1"], pk["w_pack"], cos_p, sin_p, code_p, pk["gq"], pk["gk"], bd, 512)
        cmp = _compress_prompt(nsa, batch, seq, pk["cw"] + (bd,))
        o_nsa = _attn_prompt(qc, qr, gates, cmp, kva, poolt_p, batch, seq, tq, 512)
        o_hg, hfin = _hgrn(hgin.reshape(batch, seq, 4 * HG_WIDTH), lb, pk["hg_gain"], zero_state, hg_tile, batch, BF16)
        xp = _mlp(xp, o_nsa, o_hg.reshape(batch * seq, HG_WIDTH), wo_all, pk["g2"], wu_all, wd_all, l, 512)
        p_nsa.append(nsa.reshape(batch, seq, 4, KV_HEADS, HEAD_DIM))
        nkeep = min(WINDOW, seq)
        p_win.append(win.reshape(batch, seq, 2 * KV_WIDTH)[:, seq - nkeep:].reshape(batch, nkeep, 2, KV_HEADS, HEAD_DIM))
        p_hg.append(hfin)

        ns = nb * tnew
        qc, qr, nsa, win, _, gates, hgin = _proj(xs, pk["g1"], pk["w_pack"], cos_s, sin_s, code_s, pk["gq"], pk["gk"], bd, ns)
        cmp = _compress_sample(page_table, cache_t, pk["cw"] + (bd,), l)
        r3 = lambda a: a.reshape(nb, tnew, a.shape[-1])
        o_nsa = _attn_sample(page_table, cache_t, r3(qc), r3(qr), r3(gates), cmp, r3(nsa), r3(win), winc_t,
                             pool_s, expand_s, l, tnew)
        o_hg, hfin = _hgrn(r3(hgin), lb, pk["hg_gain"], state_hgrn[l], hg_tile, math.gcd(nb, 4), F32)
        xs = _mlp(xs, o_nsa.reshape(ns, NSA_WIDTH).astype(BF16), o_hg.reshape(ns, HG_WIDTH).astype(BF16),
                  wo_all, pk["g2"], wu_all, wd_all, l, ns)
        s_nsa.append(nsa.reshape(nb, tnew, 4, KV_HEADS, HEAD_DIM))
        s_win.append(win.reshape(nb, tnew, 2, KV_HEADS, HEAD_DIM))
        s_hg.append(hfin)

    return (xp.reshape(batch, seq, D_MODEL), xs.reshape(nb, tnew, D_MODEL),
            jnp.stack(p_nsa), jnp.stack(p_win), jnp.stack(p_hg),
            jnp.stack(s_nsa), jnp.stack(s_win), jnp.stack(s_hg))
```

```python
import functools
import math

import numpy as np
import jax
import jax.numpy as jnp
from jax import lax
from jax.experimental import pallas as pl
from jax.experimental.pallas import tpu as pltpu

F32 = jnp.float32
BF16 = jnp.bfloat16

D_MODEL = 1024
HEAD_DIM = 64
NSA_HEADS = 8
KV_HEADS = 2
GROUP = NSA_HEADS // KV_HEADS
NSA_WIDTH = NSA_HEADS * HEAD_DIM
KV_WIDTH = KV_HEADS * HEAD_DIM
CMP_STRIDE = 16
CMP_HIDDEN = 4 * HEAD_DIM
SLC_BLOCK = 64
SLC_TOP_N = 16
WINDOW = 512
FORCED_SCORE = 1.0e4
HG_HEADS = 4
HG_DK = 128
HG_WIDTH = HG_HEADS * HG_DK
HG_CHUNK = 32
D_FF = 4 * D_MODEL
ROPE_THETA = 10000.0
EPS = 1e-6
SCALE = HEAD_DIM ** -0.5
LOG2E = 1.4426950408889634
RANK_UNROLL = 4
NEG = -1e30
BLOCK_CODE = -(2.0 ** 100)

LANES = 128
PA_W = NSA_WIDTH + 6 * KV_WIDTH + LANES
COL_KC, COL_VC, COL_KS, COL_VS, COL_KW, COL_VW = (NSA_WIDTH + i * KV_WIDTH for i in range(6))
COL_GATE = NSA_WIDTH + 6 * KV_WIDTH
VMEM_LIMIT = 56 * 1024 * 1024


def _cparams(sem):
    return pltpu.CompilerParams(dimension_semantics=sem, vmem_limit_bytes=VMEM_LIMIT)


def _dot(a, b):
    return jnp.dot(a, b, preferred_element_type=F32)


def _dot_nt(a, b):
    return lax.dot_general(a, b, (((1,), (1,)), ((), ())), preferred_element_type=F32)


def _split_dot(a, b, nt=False):
    hi = a.astype(BF16)
    lo = (a - hi.astype(F32)).astype(BF16)
    f = _dot_nt if nt else _dot
    return f(hi, b) + f(lo, b)


def _head_norm(v, bd, gain):
    ss = _split_dot(v * v, bd)
    return v * lax.rsqrt(ss * (1.0 / HEAD_DIM) + EPS) * gain


def _rope(v, cos, sin, first_half):
    sw = jnp.where(first_half, pltpu.roll(v, LANES - HEAD_DIM // 2, 1), pltpu.roll(v, HEAD_DIM // 2, 1))
    return v * cos + sw * sin


def _sigmoid(x):
    return 1.0 / (1.0 + jnp.exp(-x))


def _proj_body(x_ref, g_ref, w_ref, cos_ref, sin_ref, code_ref, gq_ref, gk_ref, bd_ref,
               qc_ref, qr_ref, nsa_ref, win_ref, kva_ref, gate_ref, hgin_ref):
    x = x_ref[...]
    ms = jnp.mean(x * x, axis=-1, keepdims=True)
    h = (x * lax.rsqrt(ms + EPS) * g_ref[...]).astype(BF16)
    pa = _dot(h, w_ref[:, :PA_W])
    hgin_ref[...] = _dot(h, w_ref[:, PA_W:])
    cos = cos_ref[...]
    sin = sin_ref[...]
    bd = bd_ref[...]
    lane = lax.broadcasted_iota(jnp.int32, cos.shape, 1)
    first_half = (lane % HEAD_DIM) < (HEAD_DIM // 2)
    gq = gq_ref[...]
    for c in range(NSA_WIDTH // LANES):
        sl = slice(LANES * c, LANES * (c + 1))
        qn = _head_norm(pa[:, sl], bd, gq) * (SCALE * LOG2E)
        qc_ref[:, sl] = qn.astype(BF16)
        qr_ref[:, sl] = _rope(qn, cos, sin, first_half).astype(BF16)
    kc = pa[:, COL_KC:COL_KC + KV_WIDTH]
    vc = pa[:, COL_VC:COL_VC + KV_WIDTH]
    ks = _rope(_head_norm(pa[:, COL_KS:COL_KS + KV_WIDTH], bd, gk_ref[0:1, :]), cos, sin, first_half)
    vs = pa[:, COL_VS:COL_VS + KV_WIDTH]
    kw = _rope(_head_norm(pa[:, COL_KW:COL_KW + KV_WIDTH], bd, gk_ref[1:2, :]), cos, sin, first_half)
    vw = pa[:, COL_VW:COL_VW + KV_WIDTH]
    for i, a in enumerate((kc, vc, ks, vs)):
        nsa_ref[:, LANES * i:LANES * (i + 1)] = a
    win_ref[:, 0:LANES] = kw
    win_ref[:, LANES:2 * LANES] = vw
    low = lane < HEAD_DIM
    code = code_ref[...]
    aug = []
    for kk, vv, fill in ((ks, vs, code), (kw, vw, 0.0)):
        aug += [jnp.where(low, kk, fill), jnp.where(low, pltpu.roll(kk, HEAD_DIM, 1), fill),
                jnp.where(low, vv, 1.0), jnp.where(low, pltpu.roll(vv, HEAD_DIM, 1), 1.0)]
    for i, a in enumerate(aug):
        kva_ref[:, LANES * i:LANES * (i + 1)] = a.astype(BF16)
    gate_ref[...] = _sigmoid(pa[:, COL_GATE:COL_GATE + LANES])


def _proj(x, g, w, cos, sin, code, gq, gk, bd, tn):
    n = x.shape[0]
    nt = cos.shape[0] // tn
    wtot = w.shape[1]
    row = lambda i: (i, 0)
    const = lambda i: (0, 0)
    outs = [
        jax.ShapeDtypeStruct((n, NSA_WIDTH), BF16),
        jax.ShapeDtypeStruct((n, NSA_WIDTH), BF16),
        jax.ShapeDtypeStruct((n, 4 * KV_WIDTH), F32),
        jax.ShapeDtypeStruct((n, 2 * KV_WIDTH), F32),
        jax.ShapeDtypeStruct((n, 8 * LANES), BF16),
        jax.ShapeDtypeStruct((n, LANES), F32),
        jax.ShapeDtypeStruct((n, 4 * HG_WIDTH), F32),
    ]
    return pl.pallas_call(
        _proj_body,
        grid=(n // tn,),
        in_specs=[
            pl.BlockSpec((tn, D_MODEL), row),
            pl.BlockSpec((1, D_MODEL), const),
            pl.BlockSpec((D_MODEL, wtot), const),
            pl.BlockSpec((tn, LANES), lambda i: (i % nt, 0)),
            pl.BlockSpec((tn, LANES), lambda i: (i % nt, 0)),
            pl.BlockSpec((tn, LANES), lambda i: (i % nt, 0)),
            pl.BlockSpec((1, LANES), const),
            pl.BlockSpec((2, LANES), const),
            pl.BlockSpec((LANES, LANES), const),
        ],
        out_specs=[pl.BlockSpec((tn, o.shape[1]), row) for o in outs],
        out_shape=outs,
        compiler_params=_cparams(("parallel",)),
        name="proj",
    )(x, g, w, cos, sin, code, gq, gk, bd)


def _gelu_tanh(x):
    return 0.5 * x * (1.0 + jnp.tanh(0.7978845608028654 * (x + 0.044715 * (x * x * x))))


def _fill_flat(load_rows, flat_ref, nch):
    for j in range(CMP_STRIDE):
        for slot in range(2):
            rows = load_rows(j, slot).astype(flat_ref.dtype)
            for g in range(KV_HEADS):
                flat_ref[2 * slot + g, :, HEAD_DIM * j:HEAD_DIM * (j + 1)] = rows[:, HEAD_DIM * g:HEAD_DIM * (g + 1)]


def _compress_core(flat_ref, nch, nvalid, pe_ref, w1_ref, b1_ref, w2_ref, kn_ref, bd_ref, out_ref):
    rowid = lax.broadcasted_iota(jnp.int32, (nch, LANES), 0)
    low = lax.broadcasted_iota(jnp.int32, (nch, LANES), 1) < HEAD_DIM
    for slot in range(2):
        w1 = w1_ref[slot]
        pe = pe_ref[slot]
        pw = _split_dot(pe, w1)
        const = pw[0:1, :CMP_HIDDEN] + pw[8:9, CMP_HIDDEN:] + b1_ref[slot]
        for g in range(KV_HEADS):
            f = flat_ref[2 * slot + g].astype(BF16)
            ab = _dot(f, w1)
            pre = ab[:, :CMP_HIDDEN] + pltpu.roll(ab[:, CMP_HIDDEN:], nch - 1, 0) + const
            hid = _gelu_tanh(pre).astype(BF16)
            o = _dot(hid, w2_ref[slot])
            if slot == 0:
                o = _head_norm(o, bd_ref[...], kn_ref[...])
            else:
                o = jnp.where(low, o, 1.0)
            out_ref[2 * slot + g] = jnp.where(rowid < nvalid, o, 0.0).astype(BF16)


def _compress_prompt_body(xk_ref, xv_ref, pe_ref, w1_ref, b1_ref, w2_ref, kn_ref, bd_ref, out_ref, flat_ref):
    nch = flat_ref.shape[1]
    _fill_flat(lambda j, slot: (xk_ref, xv_ref)[slot][pl.ds(j, nch, stride=CMP_STRIDE), :], flat_ref, nch)
    _compress_core(flat_ref, nch, nch - 1, pe_ref, w1_ref, b1_ref, w2_ref, kn_ref, bd_ref, out_ref)


def _cmp_weight_specs(nidx):
    z = lambda *a: (0,) * nidx
    return [
        pl.BlockSpec((2, 16, CMP_STRIDE * HEAD_DIM), lambda *a: (0, 0, 0)),
        pl.BlockSpec((2, CMP_STRIDE * HEAD_DIM, 2 * CMP_HIDDEN), lambda *a: (0, 0, 0)),
        pl.BlockSpec((2, 1, CMP_HIDDEN), lambda *a: (0, 0, 0)),
        pl.BlockSpec((2, CMP_HIDDEN, LANES), lambda *a: (0, 0, 0)),
        pl.BlockSpec((1, LANES), lambda *a: (0, 0)),
        pl.BlockSpec((LANES, LANES), lambda *a: (0, 0)),
    ]


def _compress_prompt(nsa, batch, seq, cw):
    nch = seq // CMP_STRIDE
    return pl.pallas_call(
        _compress_prompt_body,
        grid=(batch,),
        in_specs=[pl.BlockSpec((seq, KV_WIDTH), lambda b: (b, 0)),
                  pl.BlockSpec((seq, KV_WIDTH), lambda b: (b, 1))] + _cmp_weight_specs(1),
        out_specs=pl.BlockSpec((None, 4, nch, LANES), lambda b: (b, 0, 0, 0)),
        out_shape=jax.ShapeDtypeStruct((batch, 4, nch, LANES), BF16),
        scratch_shapes=[pltpu.VMEM((4, nch, CMP_STRIDE * HEAD_DIM), BF16)],
        compiler_params=_cparams(("parallel",)),
        name="compress_prompt",
    )(nsa, nsa, *cw)


def _softmax_masked(s, mask):
    sm = jnp.where(mask, s, NEG)
    m = jnp.max(sm, axis=-1, keepdims=True)
    e = jnp.where(mask, jnp.exp2(sm - m), 0.0)
    return e / jnp.maximum(jnp.sum(e, axis=-1, keepdims=True), 1e-30)


def _stack_heads(q, g, rows_f32=False):
    parts = [q[:, HEAD_DIM * (GROUP * g + n):HEAD_DIM * (GROUP * g + n + 1)] for n in range(GROUP)]
    return jnp.concatenate(parts, axis=0)


def _place_heads(q, g, fill, low):
    parts = []
    for n in range(GROUP):
        hd = GROUP * g + n
        chunk = q[:, LANES * (hd // 2):LANES * (hd // 2 + 1)].astype(F32)
        if hd % 2:
            chunk = pltpu.roll(chunk, HEAD_DIM, 1)
        parts.append(jnp.where(low, chunk, fill).astype(BF16))
    return jnp.concatenate(parts, axis=0)


def _attend(q4, k, v, bias):
    s = _add_bias(_dot_nt(q4, k), bias)
    e = jnp.exp2(s - jnp.max(s, axis=-1, keepdims=True))
    return _dot(e.astype(BF16), v), e


def _add_bias(s, bias):
    if bias is None:
        return s
    tq, width = bias.shape
    return (s.reshape(GROUP, tq, width) + bias[None]).reshape(GROUP * tq, width)


def _attn_prompt_body(qc_ref, qr_ref, gate_ref, cmp_ref, slc_ref, win_ref, poolt_ref, o_ref, score_ref,
                      *, tq, seq, kchunk):
    i = pl.program_id(1)
    q0 = i * tq
    ncmp = cmp_ref.shape[1]
    nsel = seq // SLC_BLOCK
    m4 = GROUP * tq
    qc = qc_ref[...]
    qr = qr_ref[...]
    gates = gate_ref[...]
    low = lax.broadcasted_iota(jnp.int32, (tq, LANES), 1) < HEAD_DIM
    qpos = q0 + lax.broadcasted_iota(jnp.int32, (tq, 1), 0)
    rep4 = lambda a: jnp.concatenate([a] * GROUP, axis=0)

    c_end = lax.broadcasted_iota(jnp.int32, (1, ncmp), 1) * CMP_STRIDE + (2 * CMP_STRIDE - 1)
    cmp_bias = jnp.where(c_end <= qpos, 0.0, NEG)
    cmp_any = rep4(jnp.where(qpos >= 2 * CMP_STRIDE - 1, 1.0, 0.0))
    wlen = WINDOW + tq
    w0 = pl.multiple_of(jnp.maximum(q0 - WINDOW, 0), tq)
    diff = qpos - (w0 + lax.broadcasted_iota(jnp.int32, (1, wlen), 1))
    win_bias = jnp.where((diff >= 0) & (diff < WINDOW), 0.0, NEG)
    nk = (q0 + tq + kchunk - 1) // kchunk
    kpos_d = (nk - 1) * kchunk + lax.broadcasted_iota(jnp.int32, (1, kchunk), 1)
    diag_bias = jnp.where(kpos_d <= qpos, 0.0, NEG)

    blk = lax.broadcasted_iota(jnp.int32, (nsel, tq), 0)
    qp_l = q0 + lax.broadcasted_iota(jnp.int32, (nsel, tq), 1)
    cur = qp_l // SLC_BLOCK
    forced = (blk == 0) | (blk == cur) | (blk == cur - 1)
    visible = blk * SLC_BLOCK <= qp_l
    nvis = jnp.minimum((q0 + tq + SLC_BLOCK - 1) // SLC_BLOCK, nsel)

    kcol = lambda g: slice(LANES * g, LANES * (g + 1))
    vcol = lambda g: slice(LANES * (KV_HEADS + g), LANES * (KV_HEADS + g + 1))

    for g in range(KV_HEADS):
        acc_c, e_c = _attend(_place_heads(qc, g, 0.0, low), cmp_ref[g], cmp_ref[KV_HEADS + g], cmp_bias)
        inv_c = cmp_any / acc_c[:, HEAD_DIM:HEAD_DIM + 1]
        o_cmp = acc_c * inv_c
        p = e_c * inv_c
        psum = p[0:tq] + p[tq:2 * tq] + p[2 * tq:3 * tq] + p[3 * tq:4 * tq]

        p_hi = psum.astype(BF16)
        p_lo = (psum - p_hi.astype(F32)).astype(BF16)
        imp_t = _dot_nt(poolt_ref[...], p_hi) + _dot_nt(poolt_ref[...], p_lo)
        score = jnp.where(visible, jnp.where(forced, FORCED_SCORE, imp_t), -1.0)
        score_ref[...] = score

        def rank_step(kk, rank):
            for u in range(RANK_UNROLL):
                k = kk * RANK_UNROLL + u
                rk = score_ref[pl.ds(k, 1), :]
                rank = rank + jnp.where((rk > score) | ((rk == score) & (blk > k)), 1.0, 0.0)
            return rank

        rank = lax.fori_loop(0, (nvis + RANK_UNROLL - 1) // RANK_UNROLL, rank_step, jnp.zeros((nsel, tq), F32))
        notsel = jnp.where(visible & (rank < float(SLC_TOP_N)), 0.0, 1.0)
        pieces = [jnp.zeros((HEAD_DIM, tq), F32), notsel]
        if nsel < HEAD_DIM:
            pieces.append(jnp.ones((HEAD_DIM - nsel, tq), F32))
        fill = jnp.concatenate(pieces, axis=0).T

        q4r = _place_heads(qr, g, fill, low)

        def absorb(c, m_i, acc, bias):
            k0 = pl.multiple_of(c * kchunk, kchunk)
            s = _add_bias(_dot_nt(q4r, slc_ref[pl.ds(k0, kchunk), kcol(g)]), bias)
            m_new = jnp.maximum(m_i, jnp.max(s, axis=-1, keepdims=True))
            e = jnp.exp2(s - m_new).astype(BF16)
            return m_new, jnp.exp2(m_i - m_new) * acc + _dot(e, slc_ref[pl.ds(k0, kchunk), vcol(g)])

        st = (jnp.full((m4, 1), NEG, F32), jnp.zeros((m4, LANES), F32))
        m_l, acc_l = lax.fori_loop(0, nk - 1, lambda c, cr: absorb(c, cr[0], cr[1], None), st)
        _, acc_s = absorb(nk - 1, m_l, acc_l, diag_bias)
        o_slc = acc_s / acc_s[:, HEAD_DIM:HEAD_DIM + 1]

        acc_w, _ = _attend(q4r, win_ref[pl.ds(w0, wlen), kcol(g)], win_ref[pl.ds(w0, wlen), vcol(g)], win_bias)
        o_win = acc_w / acc_w[:, HEAD_DIM:HEAD_DIM + 1]

        comb = []
        for n in range(GROUP):
            hd = GROUP * g + n
            r = slice(n * tq, (n + 1) * tq)
            comb.append(o_cmp[r] * gates[:, 3 * hd:3 * hd + 1] + o_slc[r] * gates[:, 3 * hd + 1:3 * hd + 2]
                        + o_win[r] * gates[:, 3 * hd + 2:3 * hd + 3])
        for pair in range(GROUP // 2):
            both = jnp.where(low, comb[2 * pair], pltpu.roll(comb[2 * pair + 1], HEAD_DIM, 1))
            c0 = LANES * ((GROUP // 2) * g + pair)
            o_ref[:, c0:c0 + LANES] = both.astype(BF16)


def _attn_prompt(qc, qr, gates, cmp, kva, poolt, batch, seq, tq, kchunk):
    n = batch * seq
    nq = seq // tq
    ncmp = cmp.shape[2]
    nsel = seq // SLC_BLOCK
    assert nsel <= HEAD_DIM, "block codes use the 64 spare contraction lanes"
    body = functools.partial(_attn_prompt_body, tq=tq, seq=seq, kchunk=kchunk)
    return pl.pallas_call(
        body,
        grid=(batch, nq),
        in_specs=[
            pl.BlockSpec((tq, NSA_WIDTH), lambda b, i: (b * nq + i, 0)),
            pl.BlockSpec((tq, NSA_WIDTH), lambda b, i: (b * nq + i, 0)),
            pl.BlockSpec((tq, LANES), lambda b, i: (b * nq + i, 0)),
            pl.BlockSpec((None, 4, ncmp, LANES), lambda b, i: (b, 0, 0, 0)),
            pl.BlockSpec((seq, 4 * LANES), lambda b, i: (b, 0)),
            pl.BlockSpec((seq, 4 * LANES), lambda b, i: (b, 1)),
            pl.BlockSpec((nsel, ncmp), lambda b, i: (0, 0)),
        ],
        out_specs=pl.BlockSpec((tq, NSA_WIDTH), lambda b, i: (b * nq + i, 0)),
        out_shape=jax.ShapeDtypeStruct((n, NSA_WIDTH), BF16),
        scratch_shapes=[pltpu.VMEM((nsel, tq), F32)],
        compiler_params=_cparams(("parallel", "parallel")),
        name="attn_prompt",
    )(qc, qr, gates, cmp, kva, kva, poolt)


def _hgrn_body(x_ref, lb_ref, gain_ref, s0_ref, o_ref, sfin_ref, st_ref, *, tile, nvalid, bstep):
    i = pl.program_id(1)
    nsub = tile // HG_CHUNK

    @pl.when(i == 0)
    def _():
        for bb in range(bstep):
            for hd in range(HG_HEADS):
                st_ref[bb * HG_HEADS + hd] = s0_ref[bb, hd].T

    for bb in range(bstep):
        _hgrn_tile(x_ref.at[bb], lb_ref, gain_ref, o_ref.at[bb], st_ref, bb * HG_HEADS, tile, nvalid, nsub)

    @pl.when(i == pl.num_programs(1) - 1)
    def _():
        for bb in range(bstep):
            for hd in range(HG_HEADS):
                sfin_ref[bb, hd] = st_ref[bb * HG_HEADS + hd].T


def _hgrn_tile(x_ref, lb_ref, gain_ref, o_ref, st_ref, st0, tile, nvalid, nsub):
    if nvalid < tile:
        x_all = jnp.concatenate([x_ref[...], jnp.zeros((tile - nvalid, 4 * HG_WIDTH), F32)], axis=0)
    row = lax.broadcasted_iota(jnp.int32, (tile, HG_DK), 0)
    rowc = row % HG_CHUNK
    col = lax.broadcasted_iota(jnp.int32, (tile, tile), 1)
    rr = lax.broadcasted_iota(jnp.int32, (tile, tile), 0)
    intra_mask = (rr // HG_CHUNK == col // HG_CHUNK) & (rr >= col)
    lane_t = lax.broadcasted_iota(jnp.int32, (HG_DK, tile), 1)
    valid = row < nvalid

    for hd in range(HG_HEADS):
        sl = lambda k: slice(k * HG_WIDTH + hd * HG_DK, k * HG_WIDTH + (hd + 1) * HG_DK)
        src = x_ref if nvalid == tile else x_all
        q = src[:, sl(0)]
        z = src[:, sl(1)]
        v = src[:, sl(2)]
        gt = src[:, sl(3)]
        lb = lb_ref[:, hd * HG_DK:(hd + 1) * HG_DK]
        w = jnp.exp(-jnp.abs(z))
        pos = z >= 0.0
        log_f = jnp.log(jnp.where(pos, 1.0 + lb * w, w + lb)) - jnp.log(1.0 + w)
        key = (1.0 - lb) * (jnp.where(pos, w, 1.0) / (1.0 + w))
        if nvalid < tile:
            log_f = jnp.where(valid, log_f, 0.0)
            key = jnp.where(valid, key, 0.0)
            v = jnp.where(valid, v, 0.0)
        qs = q * _sigmoid(q)
        b = log_f
        step = 1
        while step < HG_CHUNK:
            b = b + jnp.where(rowc >= step, pltpu.roll(b, step, 0), 0.0)
            step *= 2
        qe = (qs * jnp.exp(b)).astype(BF16)
        ke = (key * jnp.exp(-b)).astype(BF16)
        vb = v.astype(BF16)
        a_mat = jnp.where(intra_mask, _dot_nt(qe, ke), 0.0)
        o_acc = _dot(a_mat.astype(BF16), vb)
        vt = v.T
        b_lasts = [b[(c + 1) * HG_CHUNK - 1:(c + 1) * HG_CHUNK, :] for c in range(nsub)]
        b_end = jnp.broadcast_to(b_lasts[0], b.shape)
        for c in range(1, nsub):
            b_end = jnp.where(row >= c * HG_CHUNK, b_lasts[c], b_end)
        kd = (key * jnp.exp(b_end - b)).astype(BF16)
        st = st_ref[st0 + hd]
        o_parts = []
        for c in range(nsub):
            r = slice(c * HG_CHUNK, (c + 1) * HG_CHUNK)
            o_parts.append(_dot_nt(qe[r], st.astype(BF16)))
            vtc = jnp.where(lane_t // HG_CHUNK == c, vt, 0.0).astype(BF16)
            st = st * jnp.exp(b_lasts[c]) + _dot(vtc, kd)
        st_ref[st0 + hd] = st
        o = o_acc + jnp.concatenate(o_parts, axis=0)
        ms = jnp.mean(o * o, axis=-1, keepdims=True)
        o = o * lax.rsqrt(ms + EPS) * gain_ref[...] * _sigmoid(gt)
        o_ref[:, hd * HG_DK:(hd + 1) * HG_DK] = o[:nvalid].astype(o_ref.dtype)


def _hgrn(hgin, lb, gain, s0, tile, bstep, out_dtype):
    batch, rows, _ = hgin.shape
    nvalid = min(rows, tile)
    nt = rows // nvalid
    body = functools.partial(_hgrn_body, tile=tile, nvalid=nvalid, bstep=bstep)
    s_spec = pl.BlockSpec((bstep, HG_HEADS, HG_DK, HG_DK), lambda b, i: (b, 0, 0, 0))
    return pl.pallas_call(
        body,
        grid=(batch // bstep, nt),
        in_specs=[
            pl.BlockSpec((bstep, nvalid, 4 * HG_WIDTH), lambda b, i: (b, i, 0)),
            pl.BlockSpec((1, HG_WIDTH), lambda b, i: (0, 0)),
            pl.BlockSpec((1, HG_DK), lambda b, i: (0, 0)),
            s_spec,
        ],
        out_specs=[pl.BlockSpec((bstep, nvalid, HG_WIDTH), lambda b, i: (b, i, 0)), s_spec],
        out_shape=[jax.ShapeDtypeStruct((batch, rows, HG_WIDTH), out_dtype),
                   jax.ShapeDtypeStruct((batch, HG_HEADS, HG_DK, HG_DK), F32)],
        scratch_shapes=[pltpu.VMEM((bstep * HG_HEADS, HG_DK, HG_DK), F32)],
        compiler_params=_cparams(("parallel", "arbitrary")),
        name="hgrn",
    )(hgin, lb, gain, s0)


def _mlp_body(x_ref, on_ref, oh_ref, wo_ref, g_ref, wu_ref, wd_ref, y_ref, *, fchunk):
    x2 = x_ref[...] + _dot(jnp.concatenate([on_ref[...], oh_ref[...]], axis=1), wo_ref[...])
    ms = jnp.mean(x2 * x2, axis=-1, keepdims=True)
    h2 = (x2 * lax.rsqrt(ms + EPS) * g_ref[...]).astype(BF16)
    acc = x2
    for c in range(D_FF // fchunk):
        u = _dot(h2, wu_ref[:, c * fchunk:(c + 1) * fchunk])
        u = jnp.maximum(u, 0.0)
        acc = acc + _dot((u * u).astype(BF16), wd_ref[c * fchunk:(c + 1) * fchunk, :])
    y_ref[...] = acc


def _mlp(x, on, oh, wo, g, wu, wd, layer, tn, fchunk=1024):
    n = x.shape[0]
    row = lambda i: (i, 0)
    const = lambda i: (0, 0)
    slab = lambda i: (layer, 0, 0)
    return pl.pallas_call(
        functools.partial(_mlp_body, fchunk=fchunk),
        grid=(n // tn,),
        in_specs=[
            pl.BlockSpec((tn, D_MODEL), row),
            pl.BlockSpec((tn, NSA_WIDTH), row),
            pl.BlockSpec((tn, HG_WIDTH), row),
            pl.BlockSpec((None, D_MODEL, D_MODEL), slab),
            pl.BlockSpec((1, D_MODEL), const),
            pl.BlockSpec((None, D_MODEL, D_FF), slab),
            pl.BlockSpec((None, D_FF, D_MODEL), slab),
        ],
        out_specs=pl.BlockSpec((tn, D_MODEL), row),
        out_shape=jax.ShapeDtypeStruct((n, D_MODEL), F32),
        compiler_params=_cparams(("parallel",)),
        name="mlp",
    )(x, on, oh, wo, g, wu, wd)


def _pages_pipeline(page_copy, npages):
    b = pl.program_id(0)
    nb = pl.num_programs(0)
    slot = b % 2

    def start(bb, sl):
        def go(p, c):
            page_copy(bb, p, sl).start()
            return c
        lax.fori_loop(0, npages, go, 0, unroll=8)

    @pl.when(b == 0)
    def _():
        start(0, 0)

    @pl.when(b + 1 < nb)
    def _():
        start(b + 1, 1 - slot)

    def wait(p, c):
        page_copy(b, p, slot).wait()
        return c
    lax.fori_loop(0, npages, wait, 0, unroll=8)
    return slot


def _compress_sample_body(pt_ref, cache_ref, pe_ref, w1_ref, b1_ref, w2_ref, kn_ref, bd_ref,
                          out_ref, buf_ref, rows_ref, flat_ref, sem_ref, *, layer, page, npages):
    nfeat = 2 * KV_WIDTH

    def page_copy(b, p, slot):
        return pltpu.make_async_copy(cache_ref.at[layer, pt_ref[b, p], pl.ds(0, nfeat), :],
                                     buf_ref.at[slot, p], sem_ref.at[slot])

    slot = _pages_pipeline(page_copy, npages)
    eye = (lax.broadcasted_iota(jnp.int32, (page, page), 0)
           == lax.broadcasted_iota(jnp.int32, (page, page), 1)).astype(BF16)

    def to_rows(p, c):
        x = _dot_nt(eye, buf_ref[slot, p].astype(BF16))
        r0 = pl.multiple_of(p * page, page)
        rows_ref[0, pl.ds(r0, page), :] = x[:, :KV_WIDTH]
        rows_ref[1, pl.ds(r0, page), :] = x[:, KV_WIDTH:]
        return c
    lax.fori_loop(0, npages, to_rows, 0, unroll=8)
    nch = flat_ref.shape[1]
    _fill_flat(lambda j, s: rows_ref[s, pl.ds(j, nch, stride=CMP_STRIDE), :], flat_ref, nch)
    _compress_core(flat_ref, nch, nch - 1, pe_ref, w1_ref, b1_ref, w2_ref, kn_ref, bd_ref, out_ref)


def _compress_sample(page_table, cache_t, cw, layer):
    nb, npages = page_table.shape
    page = cache_t.shape[3]
    past = npages * page
    nch = past // CMP_STRIDE
    body = functools.partial(_compress_sample_body, layer=layer, page=page, npages=npages)
    return pl.pallas_call(
        body,
        grid_spec=pltpu.PrefetchScalarGridSpec(
            num_scalar_prefetch=1,
            grid=(nb,),
            in_specs=[pl.BlockSpec(memory_space=pl.ANY)] + _cmp_weight_specs(2),
            out_specs=pl.BlockSpec((None, 4, nch, LANES), lambda b, pt: (b, 0, 0, 0)),
            scratch_shapes=[
                pltpu.VMEM((2, npages, 2 * KV_WIDTH, page), F32),
                pltpu.VMEM((2, past, KV_WIDTH), F32),
                pltpu.VMEM((4, nch, CMP_STRIDE * HEAD_DIM), BF16),
                pltpu.SemaphoreType.DMA((2,)),
            ],
        ),
        out_shape=jax.ShapeDtypeStruct((nb, 4, nch, LANES), BF16),
        compiler_params=_cparams(("arbitrary",)),
        name="compress_sample",
    )(page_table, cache_t, *cw)


def _two_part_attention(s1, mask1, v1t, s2, mask2, v2):
    s1 = jnp.where(mask1, s1, NEG)
    s2 = jnp.where(mask2, s2, NEG)
    m = jnp.maximum(jnp.max(s1, axis=-1, keepdims=True), jnp.max(s2, axis=-1, keepdims=True))
    e1 = jnp.where(mask1, jnp.exp2(s1 - m), 0.0)
    e2 = jnp.where(mask2, jnp.exp2(s2 - m), 0.0)
    l = jnp.sum(e1, axis=-1, keepdims=True) + jnp.sum(e2, axis=-1, keepdims=True)
    o = _dot_nt(e1.astype(BF16), v1t) + _dot(e2.astype(BF16), v2)
    return o / jnp.maximum(l, 1e-30)


def _attn_sample_body(pt_ref, cache_ref, qc_ref, qr_ref, gate_ref, cmp_ref, nsa_new_ref, win_new_ref, winc_ref,
                      pool_ref, e_ref, o_ref, buf_ref, sem_ref, *, layer, page, npages, tnew):
    nfeat = 2 * KV_WIDTH

    def page_copy(b, p, slot):
        return pltpu.make_async_copy(cache_ref.at[layer, pt_ref[b, p], pl.ds(nfeat, nfeat), :],
                                     buf_ref.at[slot, :, pl.ds(p * page, page)], sem_ref.at[slot])

    slot = _pages_pipeline(page_copy, npages)
    past = npages * page
    ncmp = cmp_ref.shape[1]
    nblk = past // SLC_BLOCK
    nlane = LANES * ((nblk + 1 + LANES - 1) // LANES)
    m4 = GROUP * tnew
    qc = qc_ref[...].astype(F32)
    qr = qr_ref[...].astype(F32)
    gates = gate_ref[...]
    nsa_new = nsa_new_ref[...]
    win_new = win_new_ref[...]
    pad = jnp.zeros((16 - tnew, HEAD_DIM), F32)
    new16 = lambda a, c0: jnp.concatenate([a[:, c0:c0 + HEAD_DIM], pad], axis=0).astype(BF16)
    t4 = lax.broadcasted_iota(jnp.int32, (m4, 1), 0) % tnew
    u16 = lax.broadcasted_iota(jnp.int32, (1, 16), 1)
    new_mask = (u16 <= t4) & (u16 < tnew)

    blk = lax.broadcasted_iota(jnp.int32, (tnew, nlane), 1)
    qpos = past + lax.broadcasted_iota(jnp.int32, (tnew, nlane), 0)
    cur = qpos // SLC_BLOCK
    forced = (blk == 0) | (blk == cur) | (blk == cur - 1)
    visible = (blk * SLC_BLOCK <= qpos) & (blk <= nblk)

    for g in range(KV_HEADS):
        hsl = slice(HEAD_DIM * g, HEAD_DIM * (g + 1))
        q4 = _stack_heads(qc, g).astype(BF16)
        s = _dot_nt(q4, cmp_ref[g][:, :HEAD_DIM])
        c_end = lax.broadcasted_iota(jnp.int32, (1, ncmp), 1) * CMP_STRIDE + (2 * CMP_STRIDE - 1)
        p = _softmax_masked(s, c_end <= past + t4)
        o_cmp = _dot(p.astype(BF16), cmp_ref[KV_HEADS + g][:, :HEAD_DIM])

        imp4 = _split_dot(p, pool_ref[...])
        imp = imp4[0:tnew]
        for n in range(1, GROUP):
            imp = imp + imp4[n * tnew:(n + 1) * tnew]
        if nlane > nblk:
            imp = jnp.concatenate([imp, jnp.zeros((tnew, nlane - nblk), F32)], axis=1)
        score = jnp.where(visible, jnp.where(forced, FORCED_SCORE, imp), -1.0)
        rank = jnp.zeros((tnew, nlane), F32)
        for k in range(nblk + 1):
            ck = score[:, k:k + 1]
            beats = (ck > score) | ((ck == score) & (blk > k))
            rank = rank + jnp.where(beats, 1.0, 0.0)
        sel = jnp.where(visible & (rank < float(SLC_TOP_N)), 1.0, 0.0)

        q4r = _stack_heads(qr, g).astype(BF16)
        ksp_t = buf_ref[slot, HEAD_DIM * g:HEAD_DIM * (g + 1), :].astype(BF16)
        vsp_t = buf_ref[slot, KV_WIDTH + HEAD_DIM * g:KV_WIDTH + HEAD_DIM * (g + 1), :].astype(BF16)
        sel4 = jnp.concatenate([sel[:, :nblk]] * GROUP, axis=0).astype(BF16)
        mpast = _dot(sel4, e_ref[...]) > 0.5
        ks_new = new16(nsa_new, 2 * KV_WIDTH + HEAD_DIM * g)
        vs_new = new16(nsa_new, 3 * KV_WIDTH + HEAD_DIM * g)
        cur_sel4 = jnp.concatenate([sel[:, nblk:nblk + 1]] * GROUP, axis=0) > 0.5
        o_slc = _two_part_attention(_dot(q4r, ksp_t), mpast, vsp_t,
                                    _dot_nt(q4r, ks_new), new_mask & cur_sel4, vs_new)

        nbuf = winc_ref.shape[1]
        kwp_t = winc_ref[HEAD_DIM * g:HEAD_DIM * (g + 1), :].astype(BF16)
        vwp_t = winc_ref[KV_WIDTH + HEAD_DIM * g:KV_WIDTH + HEAD_DIM * (g + 1), :].astype(BF16)
        kw_new = new16(win_new, HEAD_DIM * g)
        vw_new = new16(win_new, KV_WIDTH + HEAD_DIM * g)
        diff = (t4 + nbuf) - lax.broadcasted_iota(jnp.int32, (1, nbuf), 1)
        o_win = _two_part_attention(_dot(q4r, kwp_t), (diff >= 0) & (diff < WINDOW), vwp_t,
                                    _dot_nt(q4r, kw_new), new_mask, vw_new)

        for n in range(GROUP):
            hd = GROUP * g + n
            r = slice(n * tnew, (n + 1) * tnew)
            o = (o_cmp[r] * gates[:, 3 * hd:3 * hd + 1] + o_slc[r] * gates[:, 3 * hd + 1:3 * hd + 2]
                 + o_win[r] * gates[:, 3 * hd + 2:3 * hd + 3])
            o_ref[:, HEAD_DIM * hd:HEAD_DIM * (hd + 1)] = o


def _attn_sample(page_table, cache_t, qc, qr, gates, cmp, nsa_new, win_new, winc_t, pool, expand, layer, tnew):
    nb, npages = page_table.shape
    page = cache_t.shape[3]
    past = npages * page
    ncmp = cmp.shape[2]
    nblk = past // SLC_BLOCK
    nbuf = winc_t.shape[3]
    body = functools.partial(_attn_sample_body, layer=layer, page=page, npages=npages, tnew=tnew)
    per_b = lambda w: pl.BlockSpec((None, tnew, w), lambda b, pt: (b, 0, 0))
    return pl.pallas_call(
        body,
        grid_spec=pltpu.PrefetchScalarGridSpec(
            num_scalar_prefetch=1,
            grid=(nb,),
            in_specs=[
                pl.BlockSpec(memory_space=pl.ANY),
                per_b(NSA_WIDTH), per_b(NSA_WIDTH), per_b(LANES),
                pl.BlockSpec((None, 4, ncmp, LANES), lambda b, pt: (b, 0, 0, 0)),
                per_b(4 * KV_WIDTH), per_b(2 * KV_WIDTH),
                pl.BlockSpec((None, None, 2 * KV_WIDTH, nbuf), lambda b, pt: (layer, b, 0, 0)),
                pl.BlockSpec((ncmp, nblk), lambda b, pt: (0, 0)),
                pl.BlockSpec((nblk, past), lambda b, pt: (0, 0)),
            ],
            out_specs=per_b(NSA_WIDTH),
            scratch_shapes=[
                pltpu.VMEM((2, 2 * KV_WIDTH, past), F32),
                pltpu.SemaphoreType.DMA((2,)),
            ],
        ),
        out_shape=jax.ShapeDtypeStruct((nb, tnew, NSA_WIDTH), F32),
        compiler_params=_cparams(("arbitrary",)),
        name="attn_sample",
    )(page_table, cache_t, qc, qr, gates, cmp, nsa_new, win_new, winc_t, pool, expand)


def _rope_tables(pos):
    half = HEAD_DIM // 2
    inv_freq = ROPE_THETA ** (-jnp.arange(half, dtype=F32) / half)
    ang = pos.astype(F32)[:, None] * inv_freq[None, :]
    cos = jnp.cos(ang)
    sin = jnp.sin(ang)
    cos_t = jnp.tile(jnp.concatenate([cos, cos], axis=-1), (1, LANES // HEAD_DIM))
    sin_t = jnp.tile(jnp.concatenate([-sin, sin], axis=-1), (1, LANES // HEAD_DIM))
    return cos_t, sin_t


def _pack_layer(l, attn_norm_g, w_in, q_norm_g, k_norm_g, cmp_pos, cmp_w1, cmp_b1, cmp_w2,
                hg_norm_g, mlp_norm_g):
    w = w_in[l]
    gate_lo = COL_GATE
    gate_hi = COL_GATE + 3 * NSA_HEADS
    w_gate = jnp.pad(w[:, gate_lo:gate_hi], ((0, 0), (0, LANES - 3 * NSA_HEADS)))
    w_pack = jnp.concatenate([w[:, :gate_lo], w_gate, w[:, gate_hi:]], axis=1).astype(BF16)
    tile2 = lambda v: jnp.tile(v, LANES // HEAD_DIM)[None, :]
    half = CMP_STRIDE * HEAD_DIM
    w1 = cmp_w1[l]
    w1cat = jnp.concatenate([w1[:, :half], w1[:, half:]], axis=-1).astype(BF16)
    pe = cmp_pos[l].reshape(2, 2, 1, half)
    pe16 = jnp.broadcast_to(pe, (2, 2, 8, half)).reshape(2, 16, half)
    w2 = cmp_w2[l]
    w2pad = jnp.concatenate([w2, jnp.zeros_like(w2)], axis=-1).astype(BF16)
    return dict(
        g1=attn_norm_g[l][None, :], w_pack=w_pack, gq=tile2(q_norm_g[l]),
        gk=jnp.concatenate([tile2(k_norm_g[l, 1]), tile2(k_norm_g[l, 2])], axis=0),
        cw=(pe16, w1cat, cmp_b1[l][:, None, :], w2pad, tile2(k_norm_g[l, 0])),
        hg_gain=hg_norm_g[l][None, :], g2=mlp_norm_g[l][None, :],
    )


def _same_head_matrix():
    i = np.arange(LANES)
    return jnp.asarray((i[:, None] // HEAD_DIM) == (i[None, :] // HEAD_DIM), BF16)


def _pool_matrix(ncmp_rows, nblk):
    c = np.arange(ncmp_rows)
    j = np.arange(nblk)
    return jnp.asarray((c[:, None] // (SLC_BLOCK // CMP_STRIDE)) == j[None, :], BF16)


def _expand_matrix(nblk, nkeys):
    j = np.arange(nblk)
    s = np.arange(nkeys)
    return jnp.asarray(j[:, None] == (s[None, :] // SLC_BLOCK), BF16)


def _block_code_table(npos):
    lane = np.arange(LANES)[None, :]
    blk = (np.arange(npos) // SLC_BLOCK)[:, None]
    return jnp.asarray(np.where(lane == HEAD_DIM + blk, BLOCK_CODE, 0.0), F32)


def _feature_major(cache):
    lead = cache.shape[:-4]
    rows = cache.shape[-4]
    n = len(lead)
    perm = tuple(range(n)) + (n + 1, n + 2, n + 3, n)
    return jnp.transpose(cache, perm).reshape(*lead, -1, rows)


def kernel(x_prompt, x_sample, cache_nsa_kv, cache_win_kv, state_hgrn, page_table, attn_norm_g, w_in,
           q_norm_g, k_norm_g, cmp_pos, cmp_w1, cmp_b1, cmp_w2, hg_lb_param, hg_norm_g, w_out,
           mlp_norm_g, w_up, w_down):
    depth = w_in.shape[0]
    batch, seq, _ = x_prompt.shape
    nb, tnew, _ = x_sample.shape
    npool, page = cache_nsa_kv.shape[1], cache_nsa_kv.shape[2]
    npages = page_table.shape[1]
    past = npages * page
    nbuf = cache_win_kv.shape[2]

    lb_all = jnp.cumsum(jax.nn.softmax(hg_lb_param.astype(F32), axis=0), axis=0)
    lb_all = lb_all - lb_all[0]

    cos_p, sin_p = _rope_tables(jnp.arange(seq, dtype=jnp.int32))
    cos_s, sin_s = _rope_tables(past + jnp.arange(tnew, dtype=jnp.int32))
    cos_s = jnp.tile(cos_s, (nb, 1))
    sin_s = jnp.tile(sin_s, (nb, 1))
    bd = _same_head_matrix()
    nsel_p = seq // SLC_BLOCK
    poolt_p = _pool_matrix(seq // CMP_STRIDE, nsel_p).T
    code_p = _block_code_table(seq)
    code_s = jnp.zeros((nb * tnew, LANES), F32)
    pool_s = _pool_matrix(past // CMP_STRIDE, past // SLC_BLOCK)
    expand_s = _expand_matrix(past // SLC_BLOCK, past)

    cache_t = _feature_major(cache_nsa_kv)
    winc_t = _feature_major(cache_win_kv)
    zero_state = jnp.zeros((batch, HG_HEADS, HG_DK, HG_DK), F32)
    wo_all, wu_all, wd_all = w_out.astype(BF16), w_up.astype(BF16), w_down.astype(BF16)

    tq = 256
    hg_tile = 128
    xp = x_prompt.reshape(batch * seq, D_MODEL)
    xs = x_sample.reshape(nb * tnew, D_MODEL)
    p_nsa, p_win, p_hg, s_nsa, s_win, s_hg = [], [], [], [], [], []
    for l in range(depth):
        pk = _pack_layer(l, attn_norm_g, w_in, q_norm_g, k_norm_g, cmp_pos, cmp_w1, cmp_b1, cmp_w2,
                         hg_norm_g, mlp_norm_g)
        lb = lb_all[l][None, :]

        qc, qr, nsa, win, kva, gates, hgin = _proj(xp, pk["g1"], pk["w_pack"], cos_p, sin_p, code_p, pk["gq"], pk["gk"], bd, 512)
        cmp = _compress_prompt(nsa, batch, seq, pk["cw"] + (bd,))
        o_nsa = _attn_prompt(qc, qr, gates, cmp, kva, poolt_p, batch, seq, tq, 512)
        o_hg, hfin = _hgrn(hgin.reshape(batch, seq, 4 * HG_WIDTH), lb, pk["hg_gain"], zero_state, hg_tile, batch, BF16)
        xp = _mlp(xp, o_nsa, o_hg.reshape(batch * seq, HG_WIDTH), wo_all, pk["g2"], wu_all, wd_all, l, 512)
        p_nsa.append(nsa.reshape(batch, seq, 4, KV_HEADS, HEAD_DIM))
        nkeep = min(WINDOW, seq)
        p_win.append(win.reshape(batch, seq, 2 * KV_WIDTH)[:, seq - nkeep:].reshape(batch, nkeep, 2, KV_HEADS, HEAD_DIM))
        p_hg.append(hfin)

        ns = nb * tnew
        qc, qr, nsa, win, _, gates, hgin = _proj(xs, pk["g1"], pk["w_pack"], cos_s, sin_s, code_s, pk["gq"], pk["gk"], bd, ns)
        cmp = _compress_sample(page_table, cache_t, pk["cw"] + (bd,), l)
        r3 = lambda a: a.reshape(nb, tnew, a.shape[-1])
        o_nsa = _attn_sample(page_table, cache_t, r3(qc), r3(qr), r3(gates), cmp, r3(nsa), r3(win), winc_t,
                             pool_s, expand_s, l, tnew)
        o_hg, hfin = _hgrn(r3(hgin), lb, pk["hg_gain"], state_hgrn[l], hg_tile, math.gcd(nb, 4), F32)
        xs = _mlp(xs, o_nsa.reshape(ns, NSA_WIDTH).astype(BF16), o_hg.reshape(ns, HG_WIDTH).astype(BF16),
                  wo_all, pk["g2"], wu_all, wd_all, l, ns)
        s_nsa.append(nsa.reshape(nb, tnew, 4, KV_HEADS, HEAD_DIM))
        s_win.append(win.reshape(nb, tnew, 2, KV_HEADS, HEAD_DIM))
        s_hg.append(hfin)

    return (xp.reshape(batch, seq, D_MODEL), xs.reshape(nb, tnew, D_MODEL),
            jnp.stack(p_nsa), jnp.stack(p_win), jnp.stack(p_hg),
            jnp.stack(s_nsa), jnp.stack(s_win), jnp.stack(s_hg))
```
